```python
import jax, jax.numpy as jnp
from jax import lax
import numpy as np

D_MODEL = 4096
BATCH = 32
SEQ = 256
DEPTH = 1
DEC_BATCH = 4
DEC_SEQ = 2048
PAST_LEN = 256

GRID_W = 64
GLA_HEADS = 8
GLA_HEAD_K = 256
GLA_HEAD_V = 512
GLA_DK = GLA_HEADS * GLA_HEAD_K
GLA_DV = GLA_HEADS * GLA_HEAD_V
GATE_RANK = 16
GATE_TEMP = 16.0
CHUNK = 64
CONV_W = D_MODEL // 2
CONV_K = 31
FFN_HIDDEN = ((8 * D_MODEL + 3 * 256 - 1) // (3 * 256)) * 256
N_MOD = 6
EPS = 1e-6
IN_SIZES = (GLA_DK, GLA_DK, GLA_DV, GLA_DV, GATE_RANK, GATE_RANK, 2 * CONV_W, 2 * D_MODEL)
IN_WIDTH = 2 * GLA_DK + 2 * GLA_DV + 2 * GATE_RANK + 2 * CONV_W + 2 * D_MODEL

kernel_name = 'hybrid_gla_conformer_diffusion_step'


def rmsnorm(x, g):
    xf = x.astype(jnp.float32)
    return xf * lax.rsqrt(jnp.mean(xf * xf, axis=-1, keepdims=True) + EPS) * g.astype(jnp.float32)


def layernorm(x, g, b):
    xf = x.astype(jnp.float32)
    mu = jnp.mean(xf, axis=-1, keepdims=True)
    var = jnp.mean(jnp.square(xf - mu), axis=-1, keepdims=True)
    return (xf - mu) * lax.rsqrt(var + EPS) * g.astype(jnp.float32) + b.astype(jnp.float32)


def adaln_params(cond, w_ada, b_ada):
    mod = jax.nn.silu(cond.astype(jnp.float32)) @ w_ada + b_ada
    return jnp.split(mod[:, None, :], N_MOD, axis=-1)


def gla_chunked(q, k, v, log_a, s0):
    B, T, H, HK = q.shape
    HV = v.shape[-1]
    n = T // CHUNK

    def chunks(t):
        return t.astype(jnp.float32).reshape(B, n, CHUNK, H, t.shape[-1]).transpose(1, 0, 3, 2, 4)

    tri = jnp.tril(jnp.ones((CHUNK, CHUNK), dtype=bool))

    def step(s, inp):
        qb, kb, vb, ab = inp
        b = jnp.cumsum(ab, axis=-2)
        b_last = b[:, :, -1:, :]
        q_dec = qb * jnp.exp(b)
        scores = jnp.einsum('bhik,bhjk->bhij', q_dec, kb * jnp.exp(-b))
        scores = jnp.where(tri, scores, 0.0)
        o = jnp.einsum('bhij,bhjv->bhiv', scores, vb) + jnp.einsum('bhik,bhkv->bhiv', q_dec, s)
        s_new = jnp.exp(b_last[:, :, 0, :, None]) * s + jnp.einsum('bhjk,bhjv->bhkv', kb * jnp.exp(b_last - b), vb)
        return s_new, o

    s_fin, o = lax.scan(step, s0.astype(jnp.float32), (chunks(q), chunks(k), chunks(v), chunks(log_a)))
    return o.transpose(1, 0, 3, 2, 4).reshape(B, T, H, HV), s_fin


def depthwise_conv(u, w, b, rows):
    B, T, C = u.shape
    seqs = u if rows is None else u.reshape(B * rows, GRID_W, C)
    out = lax.conv_general_dilated(seqs, w[:, None, :].astype(seqs.dtype), (1,), [(CONV_K // 2, CONV_K // 2)],
                                   dimension_numbers=('NWC', 'WIO', 'NWC'), feature_group_count=C)
    return out.reshape(B, T, C) + b


def trunk_layer(x, mods, s0_f, s0_b, rows, p):
    B, T, _ = x.shape
    sh1, sc1, gt1, sh2, sc2, gt2 = mods
    h = rmsnorm(x, p['norm_mix_g']) * (1.0 + sc1) + sh1
    q, k, v, g, a_f, a_b, u, mg = jnp.split(h @ p['w_in'], np.cumsum(IN_SIZES)[:-1].tolist(), axis=-1)
    q = q.reshape(B, T, GLA_HEADS, GLA_HEAD_K) * GLA_HEAD_K ** -0.5
    k = k.reshape(B, T, GLA_HEADS, GLA_HEAD_K)
    v = v.reshape(B, T, GLA_HEADS, GLA_HEAD_V)
    log_a_f = (jax.nn.log_sigmoid(a_f @ p['w_a2_fwd'] + p['b_a_fwd']) / GATE_TEMP).reshape(B, T, GLA_HEADS, GLA_HEAD_K)
    log_a_b = (jax.nn.log_sigmoid(a_b @ p['w_a2_bwd'] + p['b_a_bwd']) / GATE_TEMP).reshape(B, T, GLA_HEADS, GLA_HEAD_K)
    o_f, s_f = gla_chunked(q, k, v, log_a_f, s0_f)
    o_b_rev, s_b = gla_chunked(jnp.flip(q, 1), jnp.flip(k, 1), jnp.flip(v, 1), jnp.flip(log_a_b, 1), s0_b)
    o = rmsnorm(o_f + jnp.flip(o_b_rev, 1), p['gla_norm_g']).reshape(B, T, GLA_DV)
    y_gla = (o * jax.nn.silu(g)) @ p['w_gla_o']
    u_a, u_g = jnp.split(u, 2, axis=-1)
    u = depthwise_conv(u_a * jax.nn.sigmoid(u_g), p['conv_w'], p['conv_b'], rows)
    y_conv = jax.nn.silu(layernorm(u, p['conv_ln_g'], p['conv_ln_b'])) @ p['w_conv_o']
    mg_gla, mg_conv = jnp.split(mg, 2, axis=-1)
    mix = (jax.nn.sigmoid(mg_gla) * y_gla + jax.nn.sigmoid(mg_conv) * y_conv) @ p['w_out']
    x = x + gt1 * mix
    h2 = rmsnorm(x, p['norm_ffn_g']) * (1.0 + sc2) + sh2
    ffn = (jax.nn.silu(h2 @ p['w_ffn1']) * (h2 @ p['w_ffn3'])) @ p['w_ffn2']
    x = x + gt2 * ffn
    return x, s_f, s_b


def setup_inputs(seed: int = 0) -> dict:
    key = jax.random.key(seed)
    ks = jax.random.split(key, 27)
    L, D = DEPTH, D_MODEL

    def nrm(k, shape, scale):
        return jax.random.normal(k, shape, jnp.float32) * scale

    state_shape = (DEC_BATCH, DEPTH, GLA_HEADS, GLA_HEAD_K, GLA_HEAD_V)
    return {
        'x_prompt': nrm(ks[0], (BATCH, SEQ, D), 1.0),
        'x_sample': nrm(ks[1], (DEC_BATCH, DEC_SEQ, D), 1.0),
        'state_gla_fwd': nrm(ks[2], state_shape, 1.0),
        'state_gla_bwd': nrm(ks[3], state_shape, 1.0),
        'c': nrm(ks[4], (DEC_BATCH, D), 1.0),
        'c_ctx': nrm(ks[5], (D,), 1.0),
        'norm_mix_g': 1.0 + nrm(ks[6], (L, D), 0.01),
        'w_ada': nrm(ks[7], (L, D, N_MOD * D), 0.5 * D ** -0.5),
        'b_ada': nrm(ks[8], (L, N_MOD * D), 0.01),
        'w_in': nrm(ks[9], (L, D, IN_WIDTH), D ** -0.5),
        'w_a2_fwd': nrm(ks[10], (L, GATE_RANK, GLA_DK), GATE_RANK ** -0.5),
        'b_a_fwd': nrm(ks[11], (L, GLA_DK), 0.1),
        'w_a2_bwd': nrm(ks[12], (L, GATE_RANK, GLA_DK), GATE_RANK ** -0.5),
        'b_a_bwd': nrm(ks[13], (L, GLA_DK), 0.1),
        'gla_norm_g': 1.0 + nrm(ks[14], (L, GLA_HEAD_V), 0.01),
        'w_gla_o': nrm(ks[15], (L, GLA_DV, D), GLA_DV ** -0.5),
        'conv_w': nrm(ks[16], (L, CONV_K, CONV_W), CONV_K ** -0.5),
        'conv_b': nrm(ks[17], (L, CONV_W), 0.01),
        'conv_ln_g': 1.0 + nrm(ks[18], (L, CONV_W), 0.01),
        'conv_ln_b': nrm(ks[19], (L, CONV_W), 0.01),
        'w_conv_o': nrm(ks[20], (L, CONV_W, D), CONV_W ** -0.5),
        'w_out': nrm(ks[21], (L, D, D), D ** -0.5),
        'norm_ffn_g': 1.0 + nrm(ks[22], (L, D), 0.01),
        'w_ffn1': nrm(ks[23], (L, D, FFN_HIDDEN), D ** -0.5),
        'w_ffn3': nrm(ks[24], (L, D, FFN_HIDDEN), D ** -0.5),
        'w_ffn2': nrm(ks[25], (L, FFN_HIDDEN, D), FFN_HIDDEN ** -0.5),
        'final_norm_g': 1.0 + nrm(ks[26], (D,), 0.01),
    }


def reference(x_prompt, x_sample, state_gla_fwd, state_gla_bwd, c, c_ctx, norm_mix_g, w_ada, b_ada, w_in,
              w_a2_fwd, b_a_fwd, w_a2_bwd, b_a_bwd, gla_norm_g, w_gla_o, conv_w, conv_b, conv_ln_g, conv_ln_b,
              w_conv_o, w_out, norm_ffn_g, w_ffn1, w_ffn3, w_ffn2, final_norm_g):
    rows = x_sample.shape[1] // GRID_W
    n_req = x_prompt.shape[0]
    xp, xs = x_prompt, x_sample
    new_f, new_b = [], []
    for l in range(DEPTH):
        p = {
            'norm_mix_g': norm_mix_g[l], 'w_in': w_in[l],
            'w_a2_fwd': w_a2_fwd[l], 'b_a_fwd': b_a_fwd[l], 'w_a2_bwd': w_a2_bwd[l], 'b_a_bwd': b_a_bwd[l],
            'gla_norm_g': gla_norm_g[l], 'w_gla_o': w_gla_o[l],
            'conv_w': conv_w[l], 'conv_b': conv_b[l], 'conv_ln_g': conv_ln_g[l], 'conv_ln_b': conv_ln_b[l],
            'w_conv_o': w_conv_o[l], 'w_out': w_out[l], 'norm_ffn_g': norm_ffn_g[l],
            'w_ffn1': w_ffn1[l], 'w_ffn3': w_ffn3[l], 'w_ffn2': w_ffn2[l],
        }
        mods_ctx = adaln_params(c_ctx[None, :], w_ada[l], b_ada[l])
        mods_lat = adaln_params(c, w_ada[l], b_ada[l])
        zero_state = jnp.zeros((n_req, GLA_HEADS, GLA_HEAD_K, GLA_HEAD_V), jnp.float32)
        xp, s_f, s_b = trunk_layer(xp, mods_ctx, zero_state, zero_state, None, p)
        new_f.append(s_f)
        new_b.append(s_b)
        xs, _, _ = trunk_layer(xs, mods_lat, state_gla_fwd[:, l], state_gla_bwd[:, l], rows, p)
    y_prompt = rmsnorm(xp, final_norm_g)
    y_sample = rmsnorm(xs, final_norm_g)
    new_state_gla_fwd = jnp.stack(new_f, axis=1)
    new_state_gla_bwd = jnp.stack(new_b, axis=1)
    return (y_prompt, y_sample, new_state_gla_fwd, new_state_gla_bwd)
```

```python
import functools

import jax
import jax.numpy as jnp
from jax import lax
from jax.experimental import pallas as pl
from jax.experimental.pallas import tpu as pltpu

F32 = jnp.float32
BF16 = jnp.bfloat16

D_MODEL = 4096
GLA_HEADS = 8
GLA_HEAD_K = 256
GLA_HEAD_V = 512
GLA_DK = GLA_HEADS * GLA_HEAD_K
GLA_DV = GLA_HEADS * GLA_HEAD_V
GATE_RANK = 16
GATE_TEMP = 16.0
CHUNK = 64
GRID_W = 64
CONV_W = D_MODEL // 2
CONV_K = 31
FFN_HIDDEN = 11008
N_MOD = 6
EPS = 1e-6

LANES = 128
SUBLANES = 8
VMEM_BYTES_V7X = 64 * 1024 * 1024

COL_Q = 0
COL_K = GLA_DK
COL_V = 2 * GLA_DK
COL_G = COL_V + GLA_DV
COL_UA = COL_G + GLA_DV
COL_UG = COL_UA + CONV_W
COL_MG_GLA = COL_UG + CONV_W
COL_MG_CONV = COL_MG_GLA + D_MODEL
MAIN_WIDTH = COL_MG_CONV + D_MODEL
DECAY_COL0 = 2 * GLA_DK + 2 * GLA_DV
FFN_PAD = 11264
MOD_ROWS = 8
CONV_TILE = 256
CONV_HALO = 16


def _vmem_limit(block_bytes, scratch_bytes=0, temp_bytes=0):
    return int(min(VMEM_BYTES_V7X - (2 << 20), 2 * block_bytes + scratch_bytes + temp_bytes + (4 << 20)))


def _nbytes(shape, dtype):
    n = 1
    for s in shape:
        n *= s
    return n * jnp.dtype(dtype).itemsize


def _sigmoid(x):
    return jax.nn.sigmoid(x)


def _silu(x):
    return x * jax.nn.sigmoid(x)


def _ada_kernel(c_ref, w_ref, b_ref, o_ref):
    s = _silu(c_ref[...]).astype(BF16)
    o_ref[...] = jnp.dot(s, w_ref[...].astype(BF16), preferred_element_type=F32) + b_ref[...]


def _ada(cond, w_ada, b_ada):
    n = w_ada.shape[1]
    tn = 512
    return pl.pallas_call(
        _ada_kernel,
        grid=(n // tn,),
        in_specs=[
            pl.BlockSpec((MOD_ROWS, D_MODEL), lambda j: (0, 0)),
            pl.BlockSpec((D_MODEL, tn), lambda j: (0, j)),
            pl.BlockSpec((1, tn), lambda j: (0, j)),
        ],
        out_specs=pl.BlockSpec((MOD_ROWS, tn), lambda j: (0, j)),
        out_shape=jax.ShapeDtypeStruct((MOD_ROWS, n), F32),
        compiler_params=pltpu.CompilerParams(
            dimension_semantics=("arbitrary",),
            vmem_limit_bytes=_vmem_limit(_nbytes((D_MODEL, tn), F32), temp_bytes=_nbytes((D_MODEL, tn), BF16))),
        name="ada",
    )(cond, w_ada, b_ada.reshape(1, n))


def _norm_mod_kernel(x_ref, g_ref, sc_ref, sh_ref, o_ref):
    x = x_ref[...]
    ms = jnp.mean(x * x, axis=-1, keepdims=True)
    h = x * lax.rsqrt(ms + EPS) * g_ref[...]
    o_ref[...] = (h * (1.0 + sc_ref[...]) + sh_ref[...]).astype(o_ref.dtype)


def _rmsnorm_kernel(x_ref, g_ref, o_ref):
    x = x_ref[...]
    ms = jnp.mean(x * x, axis=-1, keepdims=True)
    o_ref[...] = x * lax.rsqrt(ms + EPS) * g_ref[...]


def _mod_spec(which, tn, row_of_tile):
    nb = D_MODEL // tn
    return pl.BlockSpec((None, 1, tn), lambda i, j, *_: (row_of_tile(i), 0, which * nb + j))


def _norm_mod(x, g, mods, which_shift, which_scale, row_of_tile, tm=256):
    m = x.shape[0]
    row = lambda i: row_of_tile(i, tm)
    return pl.pallas_call(
        _norm_mod_kernel,
        grid=(m // tm, 1),
        in_specs=[
            pl.BlockSpec((tm, D_MODEL), lambda i, j: (i, 0)),
            pl.BlockSpec((1, D_MODEL), lambda i, j: (0, 0)),
            _mod_spec(which_scale, D_MODEL, row),
            _mod_spec(which_shift, D_MODEL, row),
        ],
        out_specs=pl.BlockSpec((tm, D_MODEL), lambda i, j: (i, 0)),
        out_shape=jax.ShapeDtypeStruct((m, D_MODEL), BF16),
        compiler_params=pltpu.CompilerParams(
            dimension_semantics=("parallel", "arbitrary"),
            vmem_limit_bytes=_vmem_limit(_nbytes((tm, D_MODEL), F32) + _nbytes((tm, D_MODEL), BF16),
                                         temp_bytes=2 * _nbytes((tm, D_MODEL), F32))),
        name="norm_mod",
    )(x, g.reshape(1, D_MODEL), mods, mods)


def _final_norm(x, g, tm=256):
    m = x.shape[0]
    return pl.pallas_call(
        _rmsnorm_kernel,
        grid=(m // tm,),
        in_specs=[pl.BlockSpec((tm, D_MODEL), lambda i: (i, 0)), pl.BlockSpec((1, D_MODEL), lambda i: (0, 0))],
        out_specs=pl.BlockSpec((tm, D_MODEL), lambda i: (i, 0)),
        out_shape=jax.ShapeDtypeStruct((m, D_MODEL), F32),
        compiler_params=pltpu.CompilerParams(
            dimension_semantics=("parallel",),
            vmem_limit_bytes=_vmem_limit(2 * _nbytes((tm, D_MODEL), F32), temp_bytes=_nbytes((tm, D_MODEL), F32))),
        name="final_norm",
    )(x, g.reshape(1, D_MODEL))


def _mm_kernel(x_ref, w_ref, o_ref):
    o_ref[...] = jnp.dot(x_ref[...], w_ref[...], preferred_element_type=F32).astype(o_ref.dtype)


def _mm(x, w, out_dtype, tm, tn, name):
    m, k = x.shape
    n = w.shape[1]
    blocks = _nbytes((tm, k), BF16) + _nbytes((k, tn), BF16) + _nbytes((tm, tn), out_dtype)
    return pl.pallas_call(
        _mm_kernel,
        grid=(m // tm, n // tn),
        in_specs=[pl.BlockSpec((tm, k), lambda i, j: (i, 0)), pl.BlockSpec((k, tn), lambda i, j: (0, j))],
        out_specs=pl.BlockSpec((tm, tn), lambda i, j: (i, j)),
        out_shape=jax.ShapeDtypeStruct((m, n), out_dtype),
        compiler_params=pltpu.CompilerParams(
            dimension_semantics=("parallel", "parallel"),
            vmem_limit_bytes=_vmem_limit(blocks, temp_bytes=_nbytes((tm, tn), F32))),
        name=name,
    )(x, w)


def _merge_kernel(og_ref, uc_ref, wg_ref, wc_ref, mgg_ref, mgc_ref, o_ref):
    y_gla = jnp.dot(og_ref[...], wg_ref[...], preferred_element_type=F32)
    y_conv = jnp.dot(uc_ref[...], wc_ref[...], preferred_element_type=F32)
    mix = _sigmoid(mgg_ref[...].astype(F32)) * y_gla + _sigmoid(mgc_ref[...].astype(F32)) * y_conv
    o_ref[...] = mix.astype(o_ref.dtype)


def _merge(og, uc, proj, w_gla_o, w_conv_o, tm=1024, tn=512):
    m = og.shape[0]
    blocks = (_nbytes((tm, GLA_DV), BF16) + _nbytes((tm, CONV_W), BF16) + _nbytes((GLA_DV, tn), BF16)
              + _nbytes((CONV_W, tn), BF16) + 3 * _nbytes((tm, tn), BF16))
    return pl.pallas_call(
        _merge_kernel,
        grid=(m // tm, D_MODEL // tn),
        in_specs=[
            pl.BlockSpec((tm, GLA_DV), lambda i, j: (i, 0)),
            pl.BlockSpec((tm, CONV_W), lambda i, j: (i, 0)),
            pl.BlockSpec((GLA_DV, tn), lambda i, j: (0, j)),
            pl.BlockSpec((CONV_W, tn), lambda i, j: (0, j)),
            pl.BlockSpec((tm, tn), lambda i, j: (i, COL_MG_GLA // tn + j)),
            pl.BlockSpec((tm, tn), lambda i, j: (i, COL_MG_CONV // tn + j)),
        ],
        out_specs=pl.BlockSpec((tm, tn), lambda i, j: (i, j)),
        out_shape=jax.ShapeDtypeStruct((m, D_MODEL), BF16),
        compiler_params=pltpu.CompilerParams(
            dimension_semantics=("parallel", "parallel"),
            vmem_limit_bytes=_vmem_limit(blocks, temp_bytes=3 * _nbytes((tm, tn), F32))),
        name="merge",
    )(og, uc, w_gla_o, w_conv_o, proj, proj)


def _mm_resid_kernel(x_ref, w_ref, r_ref, gt_ref, o_ref, *acc, nk):
    part = jnp.dot(x_ref[...], w_ref[...], preferred_element_type=F32)
    if nk == 1:
        o_ref[...] = r_ref[...] + gt_ref[...] * part
        return
    acc_ref, = acc
    kk = pl.program_id(2)

    @pl.when(kk == 0)
    def _():
        acc_ref[...] = part

    @pl.when(jnp.logical_and(kk > 0, kk < nk - 1))
    def _():
        acc_ref[...] += part

    @pl.when(kk == nk - 1)
    def _():
        o_ref[...] = r_ref[...] + gt_ref[...] * (acc_ref[...] + part)


def _mm_resid(x, w, resid, mods, which_gate, row_of_tile, tm, tn, tk, name):
    m, k = x.shape
    n = w.shape[1]
    nk = k // tk
    row = lambda i: row_of_tile(i, tm)
    blocks = _nbytes((tm, tk), BF16) + _nbytes((tk, tn), BF16) + 2 * _nbytes((tm, tn), F32)
    return pl.pallas_call(
        functools.partial(_mm_resid_kernel, nk=nk),
        grid=(m // tm, n // tn, nk),
        in_specs=[
            pl.BlockSpec((tm, tk), lambda i, j, kk: (i, kk)),
            pl.BlockSpec((tk, tn), lambda i, j, kk: (kk, j)),
            pl.BlockSpec((tm, tn), lambda i, j, kk: (i, j)),
            _mod_spec(which_gate, tn, row),
        ],
        out_specs=pl.BlockSpec((tm, tn), lambda i, j, kk: (i, j)),
        out_shape=jax.ShapeDtypeStruct((m, n), F32),
        scratch_shapes=[] if nk == 1 else [pltpu.VMEM((tm, tn), F32)],
        compiler_params=pltpu.CompilerParams(
            dimension_semantics=("parallel", "parallel", "arbitrary"),
            vmem_limit_bytes=_vmem_limit(blocks, scratch_bytes=_nbytes((tm, tn), F32),
                                         temp_bytes=_nbytes((tm, tn), F32))),
        name=name,
    )(x, w, resid, mods)


def _ffn_glu_kernel(x_ref, w1_ref, w3_ref, o_ref):
    x = x_ref[...]
    a = jnp.dot(x, w1_ref[...], preferred_element_type=F32)
    b = jnp.dot(x, w3_ref[...], preferred_element_type=F32)
    o_ref[...] = (_silu(a) * b).astype(o_ref.dtype)


def _ffn_glu(x, w1, w3, tm=1024, tn=512):
    m, k = x.shape
    n = w1.shape[1]
    blocks = _nbytes((tm, k), BF16) + 2 * _nbytes((k, tn), BF16) + _nbytes((tm, tn), BF16)
    return pl.pallas_call(
        _ffn_glu_kernel,
        grid=(m // tm, n // tn),
        in_specs=[
            pl.BlockSpec((tm, k), lambda i, j: (i, 0)),
            pl.BlockSpec((k, tn), lambda i, j: (0, j)),
            pl.BlockSpec((k, tn), lambda i, j: (0, j)),
        ],
        out_specs=pl.BlockSpec((tm, tn), lambda i, j: (i, j)),
        out_shape=jax.ShapeDtypeStruct((m, n), BF16),
        compiler_params=pltpu.CompilerParams(
            dimension_semantics=("parallel", "parallel"),
            vmem_limit_bytes=_vmem_limit(blocks, temp_bytes=3 * _nbytes((tm, tn), F32))),
        name="ffn_glu",
    )(x, w1, w3)


def _split_bf16(x):
    hi = x.astype(BF16)
    lo = (x - hi.astype(F32)).astype(BF16)
    return hi, lo


def _gla_chunk(qb, kb, vb, ab, w_a2, b_a, tri, s_ref, first_row):
    z = jnp.dot(ab, w_a2, preferred_element_type=F32) + b_a
    log_a = (jnp.minimum(z, 0.0) - jnp.log1p(jnp.exp(-jnp.abs(z)))) * (1.0 / GATE_TEMP)
    hi, lo = _split_bf16(log_a)
    tri_b = tri.astype(BF16)
    b = jnp.dot(tri_b, hi, preferred_element_type=F32) + jnp.dot(tri_b, lo, preferred_element_type=F32)
    b_end = b[0:1, :] if first_row else b[CHUNK - 1:CHUNK, :]
    ones = jnp.ones((CHUNK, LANES), BF16)
    tn_dims = (((0,), (0,)), ((), ()))
    tot = (lax.dot_general(hi, ones, tn_dims, preferred_element_type=F32)
           + lax.dot_general(lo, ones, tn_dims, preferred_element_type=F32))
    decay_col = jnp.exp(tot[:, 0:1])

    qf = qb.astype(F32) * (GLA_HEAD_K ** -0.5)
    kf = kb.astype(F32)
    q_dec = (qf * jnp.exp(b)).astype(BF16)
    k_dec = (kf * jnp.exp(-b)).astype(BF16)
    k_end = (kf * jnp.exp(b_end - b)).astype(BF16)
    scores = lax.dot_general(q_dec, k_dec, (((1,), (1,)), ((), ())), preferred_element_type=F32)
    scores = jnp.where(tri, scores, 0.0).astype(BF16)
    s = s_ref[...]
    o = (jnp.dot(scores, vb, preferred_element_type=F32)
         + jnp.dot(q_dec, s.astype(BF16), preferred_element_type=F32))
    s_ref[...] = decay_col * s + lax.dot_general(k_end, vb, tn_dims, preferred_element_type=F32)
    return o


def _gla_kernel(*refs, seq_len, has_init, emit_state):
    (q_ref, k_ref, v_ref, g_ref, a_ref, waf_ref, baf_ref, wab_ref, bab_ref, gn_ref), refs = refs[:10], refs[10:]
    if has_init:
        (s0f_ref, s0b_ref), refs = refs[:2], refs[2:]
    og_ref, refs = refs[0], refs[1:]
    if emit_state:
        (sf_out_ref, sb_out_ref), refs = refs[:2], refs[2:]
    of_ref, ob_ref, sf_ref, sb_ref = refs
    n_chunks = seq_len // CHUNK

    if has_init:
        sf_ref[...] = s0f_ref[...]
        sb_ref[...] = s0b_ref[...]
    else:
        sf_ref[...] = jnp.zeros_like(sf_ref)
        sb_ref[...] = jnp.zeros_like(sb_ref)

    row = lax.broadcasted_iota(jnp.int32, (CHUNK, CHUNK), 0)
    col = lax.broadcasted_iota(jnp.int32, (CHUNK, CHUNK), 1)
    tri_f = row >= col
    tri_b = col >= row

    def scan_step(i, carry):
        tf = pl.ds(pl.multiple_of(i * CHUNK, CHUNK), CHUNK)
        tb = pl.ds(pl.multiple_of((n_chunks - 1 - i) * CHUNK, CHUNK), CHUNK)
        of_ref[tf, :] = _gla_chunk(q_ref[tf, :], k_ref[tf, :], v_ref[tf, :], a_ref[tf, :].astype(BF16),
                                   waf_ref[...], baf_ref[...], tri_f, sf_ref, first_row=False)
        ob_ref[tb, :] = _gla_chunk(q_ref[tb, :], k_ref[tb, :], v_ref[tb, :], a_ref[tb, :].astype(BF16),
                                   wab_ref[...], bab_ref[...], tri_b, sb_ref, first_row=True)
        return carry

    lax.fori_loop(0, n_chunks, scan_step, 0)

    def out_step(i, carry):
        t = pl.ds(pl.multiple_of(i * CHUNK, CHUNK), CHUNK)
        o = of_ref[t, :] + ob_ref[t, :]
        ms = jnp.mean(o * o, axis=-1, keepdims=True)
        o = o * lax.rsqrt(ms + EPS) * gn_ref[...]
        og_ref[t, :] = (o * _silu(g_ref[t, :].astype(F32))).astype(og_ref.dtype)
        return carry

    lax.fori_loop(0, n_chunks, out_step, 0)

    if emit_state:
        sf_out_ref[...] = sf_ref[...]
        sb_out_ref[...] = sb_ref[...]


def _gla(proj, a_lr, w_a2f, b_af, w_a2b, b_ab, gn, batch, seq_len, s0=None, emit_state=False):
    proj3 = proj.reshape(batch, seq_len, MAIN_WIDTH)
    a3 = a_lr.reshape(batch, seq_len, LANES)
    kq, kv = COL_K // GLA_HEAD_K, COL_V // GLA_HEAD_V
    kg = COL_G // GLA_HEAD_V
    in_specs = [
        pl.BlockSpec((None, seq_len, GLA_HEAD_K), lambda b, h: (b, 0, h)),
        pl.BlockSpec((None, seq_len, GLA_HEAD_K), lambda b, h: (b, 0, kq + h)),
        pl.BlockSpec((None, seq_len, GLA_HEAD_V), lambda b, h: (b, 0, kv + h)),
        pl.BlockSpec((None, seq_len, GLA_HEAD_V), lambda b, h: (b, 0, kg + h)),
        pl.BlockSpec((None, seq_len, LANES), lambda b, h: (b, 0, 0)),
        pl.BlockSpec((LANES, GLA_HEAD_K), lambda b, h: (0, h)),
        pl.BlockSpec((1, GLA_HEAD_K), lambda b, h: (0, h)),
        pl.BlockSpec((LANES, GLA_HEAD_K), lambda b, h: (0, h)),
        pl.BlockSpec((1, GLA_HEAD_K), lambda b, h: (0, h)),
        pl.BlockSpec((1, GLA_HEAD_V), lambda b, h: (0, 0)),
    ]
    args = [proj3, proj3, proj3, proj3, a3, w_a2f, b_af, w_a2b, b_ab, gn]
    state_spec = pl.BlockSpec((None, None, GLA_HEAD_K, GLA_HEAD_V), lambda b, h: (b, h, 0, 0))
    if s0 is not None:
        in_specs += [state_spec, state_spec]
        args += list(s0)
    out_specs = [pl.BlockSpec((None, seq_len, GLA_HEAD_V), lambda b, h: (b, 0, h))]
    out_shape = [jax.ShapeDtypeStruct((batch, seq_len, GLA_DV), BF16)]
    if emit_state:
        out_specs += [state_spec, state_spec]
        out_shape += [jax.ShapeDtypeStruct((batch, GLA_HEADS, GLA_HEAD_K, GLA_HEAD_V), F32)] * 2
    state_bytes = _nbytes((GLA_HEAD_K, GLA_HEAD_V), F32)
    blocks = (2 * _nbytes((seq_len, GLA_HEAD_K), BF16) + 3 * _nbytes((seq_len, GLA_HEAD_V), BF16)
              + _nbytes((seq_len, LANES), F32) + 4 * state_bytes)
    scratch = 2 * _nbytes((seq_len, GLA_HEAD_V), F32) + 2 * state_bytes
    outs = pl.pallas_call(
        functools.partial(_gla_kernel, seq_len=seq_len, has_init=s0 is not None, emit_state=emit_state),
        grid=(batch, GLA_HEADS),
        in_specs=in_specs,
        out_specs=out_specs,
        out_shape=out_shape,
        scratch_shapes=[
            pltpu.VMEM((seq_len, GLA_HEAD_V), F32),
            pltpu.VMEM((seq_len, GLA_HEAD_V), F32),
            pltpu.VMEM((GLA_HEAD_K, GLA_HEAD_V), F32),
            pltpu.VMEM((GLA_HEAD_K, GLA_HEAD_V), F32),
        ],
        compiler_params=pltpu.CompilerParams(
            dimension_semantics=("parallel", "arbitrary"),
            vmem_limit_bytes=_vmem_limit(blocks, scratch_bytes=scratch, temp_bytes=8 << 20)),
        name="gla",
    )(*args)
    og = outs[0].reshape(batch * seq_len, GLA_DV)
    return (og, outs[1], outs[2]) if emit_state else (og, None, None)


def _conv_kernel(ua_ref, ug_ref, cw_ref, cb_ref, lng_ref, lnb_ref, o_ref, pad_ref, conv_ref, *, seg_len):
    n_seg = CONV_TILE // seg_len
    stride = seg_len + 2 * CONV_HALO
    rows = min(seg_len, 64)
    glu = ua_ref[...].astype(F32) * _sigmoid(ug_ref[...].astype(F32))
    halo = jnp.zeros((CONV_HALO, CONV_W), F32)
    for s in range(n_seg):
        base = s * stride
        pad_ref[base:base + CONV_HALO, :] = halo
        pad_ref[base + CONV_HALO:base + CONV_HALO + seg_len, :] = glu[s * seg_len:(s + 1) * seg_len, :]
        pad_ref[base + CONV_HALO + seg_len:base + stride, :] = halo

    def lane_block(cb, carry):
        lanes = pl.ds(pl.multiple_of(cb * LANES, LANES), LANES)
        bias = cb_ref[:, lanes]
        for s in range(n_seg):
            for r in range(seg_len // rows):
                first = s * stride + CONV_HALO - CONV_K // 2 + r * rows
                acc = jnp.zeros((rows, LANES), F32)
                for tap in range(CONV_K):
                    acc = acc + pad_ref[pl.ds(first + tap, rows), lanes] * cw_ref[tap:tap + 1, lanes]
                conv_ref[pl.ds(s * seg_len + r * rows, rows), lanes] = acc + bias
        return carry

    lax.fori_loop(0, CONV_W // LANES, lane_block, 0)

    u = conv_ref[...]
    mu = jnp.mean(u, axis=-1, keepdims=True)
    uc = u - mu
    var = jnp.mean(uc * uc, axis=-1, keepdims=True)
    y = uc * lax.rsqrt(var + EPS) * lng_ref[...] + lnb_ref[...]
    o_ref[...] = _silu(y).astype(o_ref.dtype)


def _conv_module(proj, conv_w, conv_b, ln_g, ln_b, seg_len):
    m = proj.shape[0]
    n_seg = CONV_TILE // seg_len
    pad_rows = n_seg * (seg_len + 2 * CONV_HALO)
    row = lambda v: v.reshape(1, CONV_W)
    blocks = 3 * _nbytes((CONV_TILE, CONV_W), BF16)
    scratch = _nbytes((pad_rows, CONV_W), F32) + _nbytes((CONV_TILE, CONV_W), F32)
    return pl.pallas_call(
        functools.partial(_conv_kernel, seg_len=seg_len),
        grid=(m // CONV_TILE,),
        in_specs=[
            pl.BlockSpec((CONV_TILE, CONV_W), lambda i: (i, COL_UA // CONV_W)),
            pl.BlockSpec((CONV_TILE, CONV_W), lambda i: (i, COL_UG // CONV_W)),
            pl.BlockSpec((CONV_K, CONV_W), lambda i: (0, 0)),
            pl.BlockSpec((1, CONV_W), lambda i: (0, 0)),
            pl.BlockSpec((1, CONV_W), lambda i: (0, 0)),
            pl.BlockSpec((1, CONV_W), lambda i: (0, 0)),
        ],
        out_specs=pl.BlockSpec((CONV_TILE, CONV_W), lambda i: (i, 0)),
        out_shape=jax.ShapeDtypeStruct((m, CONV_W), BF16),
        scratch_shapes=[pltpu.VMEM((pad_rows, CONV_W), F32), pltpu.VMEM((CONV_TILE, CONV_W), F32)],
        compiler_params=pltpu.CompilerParams(
            dimension_semantics=("parallel",),
            vmem_limit_bytes=_vmem_limit(blocks, scratch_bytes=scratch,
                                         temp_bytes=4 * _nbytes((CONV_TILE, CONV_W), F32))),
        name="conv_module",
    )(proj, proj, conv_w, row(conv_b), row(ln_g), row(ln_b))


def _trunk(x, mods, row_of_tile, batch, seq_len, conv_seg, wts, s0, emit_state):
    h = _norm_mod(x, wts["norm_mix_g"], mods, 0, 1, row_of_tile)
    proj = _mm(h, wts["w_main"], BF16, 1024, 1024, "in_proj")
    a_lr = _mm(h, wts["w_decay"], F32, 1024, LANES, "decay_proj")
    og, s_f, s_b = _gla(proj, a_lr, wts["w_a2f"], wts["b_af"], wts["w_a2b"], wts["b_ab"], wts["gla_norm_g"],
                        batch, seq_len, s0=s0, emit_state=emit_state)
    uc = _conv_module(proj, wts["conv_w"], wts["conv_b"], wts["conv_ln_g"], wts["conv_ln_b"], conv_seg)
    mix = _merge(og, uc, proj, wts["w_gla_o"], wts["w_conv_o"])
    x1 = _mm_resid(mix, wts["w_out"], x, mods, 2, row_of_tile, 1024, 1024, D_MODEL, "out_proj")
    h2 = _norm_mod(x1, wts["norm_ffn_g"], mods, 3, 4, row_of_tile)
    hid = _ffn_glu(h2, wts["w_ffn1"], wts["w_ffn3"])
    x2 = _mm_resid(hid, wts["w_ffn2"], x1, mods, 5, row_of_tile, 1024, 1024, FFN_PAD // 4, "ffn_down")
    return _final_norm(x2, wts["final_norm_g"]), s_f, s_b


def kernel(x_prompt, x_sample, state_gla_fwd, state_gla_bwd, c, c_ctx, norm_mix_g, w_ada, b_ada, w_in, w_a2_fwd, b_a_fwd, w_a2_bwd, b_a_bwd, gla_norm_g, w_gla_o, conv_w, conv_b, conv_ln_g, conv_ln_b, w_conv_o, w_out, norm_ffn_g, w_ffn1, w_ffn3, w_ffn2, final_norm_g):
    n_req, seq, _ = x_prompt.shape
    dec_batch, dec_seq, _ = x_sample.shape
    depth = w_in.shape[0]
    assert depth == 1 and dec_batch + 1 <= MOD_ROWS

    wi = w_in[0]
    pad_ffn = FFN_PAD - FFN_HIDDEN
    decay_rows = lambda w, first: jnp.pad(w.astype(BF16), ((first, LANES - GATE_RANK - first), (0, 0)))
    wts = {
        "w_main": jnp.concatenate([wi[:, :DECAY_COL0], wi[:, DECAY_COL0 + 2 * GATE_RANK:]], axis=1).astype(BF16),
        "w_decay": jnp.pad(wi[:, DECAY_COL0:DECAY_COL0 + 2 * GATE_RANK].astype(BF16),
                           ((0, 0), (0, LANES - 2 * GATE_RANK))),
        "w_a2f": decay_rows(w_a2_fwd[0], 0),
        "w_a2b": decay_rows(w_a2_bwd[0], GATE_RANK),
        "b_af": b_a_fwd[0].reshape(1, GLA_DK),
        "b_ab": b_a_bwd[0].reshape(1, GLA_DK),
        "gla_norm_g": gla_norm_g[0].reshape(1, GLA_HEAD_V),
        "w_gla_o": w_gla_o[0].astype(BF16),
        "conv_w": conv_w[0], "conv_b": conv_b[0], "conv_ln_g": conv_ln_g[0], "conv_ln_b": conv_ln_b[0],
        "w_conv_o": w_conv_o[0].astype(BF16),
        "w_out": w_out[0].astype(BF16),
        "w_ffn1": jnp.pad(w_ffn1[0].astype(BF16), ((0, 0), (0, pad_ffn))),
        "w_ffn3": jnp.pad(w_ffn3[0].astype(BF16), ((0, 0), (0, pad_ffn))),
        "w_ffn2": jnp.pad(w_ffn2[0].astype(BF16), ((0, pad_ffn), (0, 0))),
        "norm_mix_g": norm_mix_g[0], "norm_ffn_g": norm_ffn_g[0], "final_norm_g": final_norm_g,
    }

    cond = jnp.concatenate([c_ctx[None, :], c, jnp.zeros((MOD_ROWS - 1 - dec_batch, D_MODEL), F32)], axis=0)
    mods = _ada(cond, w_ada[0], b_ada[0]).reshape(MOD_ROWS, 1, N_MOD * D_MODEL)

    ctx_row = lambda i, tm: 0
    lat_row = lambda i, tm: 1 + (i * tm) // dec_seq

    y_p, s_f, s_b = _trunk(x_prompt.reshape(n_req * seq, D_MODEL), mods, ctx_row, n_req, seq, seq, wts,
                           None, True)
    y_s, _, _ = _trunk(x_sample.reshape(dec_batch * dec_seq, D_MODEL), mods, lat_row, dec_batch, dec_seq, GRID_W,
                       wts, (state_gla_fwd[:, 0], state_gla_bwd[:, 0]), False)
    state_shape = (n_req, depth, GLA_HEADS, GLA_HEAD_K, GLA_HEAD_V)
    return (y_p.reshape(n_req, seq, D_MODEL), y_s.reshape(dec_batch, dec_seq, D_MODEL),
            s_f.reshape(state_shape), s_b.reshape(state_shape))
```

```python
import functools

import jax
import jax.numpy as jnp
from jax import lax
from jax.experimental import pallas as pl
from jax.experimental.pallas import tpu as pltpu

F32 = jnp.float32
BF16 = jnp.bfloat16

D_MODEL = 4096
GLA_HEADS = 8
GLA_HEAD_K = 256
GLA_HEAD_V = 512
GLA_DK = GLA_HEADS * GLA_HEAD_K
GLA_DV = GLA_HEADS * GLA_HEAD_V
GATE_RANK = 16
GATE_TEMP = 16.0
CHUNK = 64
GRID_W = 64
CONV_W = D_MODEL // 2
CONV_K = 31
FFN_HIDDEN = 11008
N_MOD = 6
EPS = 1e-6

LANES = 128
SUBLANES = 8
VMEM_BYTES_V7X = 64 * 1024 * 1024

COL_Q = 0
COL_K = GLA_DK
COL_V = 2 * GLA_DK
COL_G = COL_V + GLA_DV
HALF_WIDTH = COL_G + GLA_DV
COL_UA = 0
COL_UG = CONV_W
COL_MG_GLA = 2 * CONV_W
COL_MG_CONV = COL_MG_GLA + D_MODEL
assert COL_MG_CONV + D_MODEL == HALF_WIDTH
DECAY_COL0 = HALF_WIDTH
GROUP = 256
CHUNKS_PER_GROUP = GROUP // CHUNK
FFN_PAD = 11264
MOD_ROWS = 8
CONV_TILE = 256
CONV_HALO = 16


def _vmem_limit(block_bytes, scratch_bytes=0, temp_bytes=0):
    return int(min(VMEM_BYTES_V7X - (2 << 20), 2 * block_bytes + scratch_bytes + temp_bytes + (4 << 20)))


def _nbytes(shape, dtype):
    n = 1
    for s in shape:
        n *= s
    return n * jnp.dtype(dtype).itemsize


def _sigmoid(x):
    return jax.nn.sigmoid(x)


def _silu(x):
    return x * jax.nn.sigmoid(x)


def _cast_pad_kernel(x_ref, o_ref, *, valid_blocks):
    inside = jnp.logical_and(pl.program_id(0) < valid_blocks[0], pl.program_id(1) < valid_blocks[1])

    @pl.when(inside)
    def _():
        o_ref[...] = x_ref[...].astype(o_ref.dtype)

    @pl.when(jnp.logical_not(inside))
    def _():
        o_ref[...] = jnp.zeros_like(o_ref)


def _cast_pad(w, out_shape, block):
    assert all(s % b == 0 and o % b == 0 for s, o, b in zip(w.shape, out_shape, block))
    valid = (w.shape[0] // block[0], w.shape[1] // block[1])
    return pl.pallas_call(
        functools.partial(_cast_pad_kernel, valid_blocks=valid),
        grid=(out_shape[0] // block[0], out_shape[1] // block[1]),
        in_specs=[pl.BlockSpec(block, lambda i, j: (jnp.minimum(i, valid[0] - 1), jnp.minimum(j, valid[1] - 1)))],
        out_specs=pl.BlockSpec(block, lambda i, j: (i, j)),
        out_shape=jax.ShapeDtypeStruct(out_shape, BF16),
        compiler_params=pltpu.CompilerParams(
            dimension_semantics=("parallel", "parallel"),
            vmem_limit_bytes=_vmem_limit(_nbytes(block, F32) + _nbytes(block, BF16))),
        name="cast_pad",
    )(w)


def _ada_kernel(c_ref, w_ref, b_ref, o_ref):
    s = _silu(c_ref[...]).astype(BF16)
    o_ref[...] = jnp.dot(s, w_ref[...].astype(BF16), preferred_element_type=F32) + b_ref[...]


def _ada(cond, w_ada, b_ada):
    n = w_ada.shape[1]
    tn = 512
    return pl.pallas_call(
        _ada_kernel,
        grid=(n // tn,),
        in_specs=[
            pl.BlockSpec((MOD_ROWS, D_MODEL), lambda j: (0, 0)),
            pl.BlockSpec((D_MODEL, tn), lambda j: (0, j)),
            pl.BlockSpec((1, tn), lambda j: (0, j)),
        ],
        out_specs=pl.BlockSpec((MOD_ROWS, tn), lambda j: (0, j)),
        out_shape=jax.ShapeDtypeStruct((MOD_ROWS, n), F32),
        compiler_params=pltpu.CompilerParams(
            dimension_semantics=("arbitrary",),
            vmem_limit_bytes=_vmem_limit(_nbytes((D_MODEL, tn), F32), temp_bytes=_nbytes((D_MODEL, tn), BF16))),
        name="ada",
    )(cond, w_ada, b_ada.reshape(1, n))


def _norm_mod_kernel(x_ref, g_ref, sc_ref, sh_ref, o_ref):
    x = x_ref[...]
    ms = jnp.mean(x * x, axis=-1, keepdims=True)
    h = x * lax.rsqrt(ms + EPS) * g_ref[...]
    o_ref[...] = (h * (1.0 + sc_ref[...]) + sh_ref[...]).astype(o_ref.dtype)


def _rmsnorm_kernel(x_ref, g_ref, o_ref):
    x = x_ref[...]
    ms = jnp.mean(x * x, axis=-1, keepdims=True)
    o_ref[...] = x * lax.rsqrt(ms + EPS) * g_ref[...]


def _mod_spec(which, tn, row_of_tile):
    nb = D_MODEL // tn
    return pl.BlockSpec((None, 1, tn), lambda i, j, *_: (row_of_tile(i), 0, which * nb + j))


def _norm_mod(x, g, mods, which_shift, which_scale, row_of_tile, tm=256):
    m = x.shape[0]
    row = lambda i: row_of_tile(i, tm)
    return pl.pallas_call(
        _norm_mod_kernel,
        grid=(m // tm, 1),
        in_specs=[
            pl.BlockSpec((tm, D_MODEL), lambda i, j: (i, 0)),
            pl.BlockSpec((1, D_MODEL), lambda i, j: (0, 0)),
            _mod_spec(which_scale, D_MODEL, row),
            _mod_spec(which_shift, D_MODEL, row),
        ],
        out_specs=pl.BlockSpec((tm, D_MODEL), lambda i, j: (i, 0)),
        out_shape=jax.ShapeDtypeStruct((m, D_MODEL), BF16),
        compiler_params=pltpu.CompilerParams(
            dimension_semantics=("parallel", "arbitrary"),
            vmem_limit_bytes=_vmem_limit(_nbytes((tm, D_MODEL), F32) + _nbytes((tm, D_MODEL), BF16),
                                         temp_bytes=2 * _nbytes((tm, D_MODEL), F32))),
        name="norm_mod",
    )(x, g.reshape(1, D_MODEL), mods, mods)


def _final_norm(x, g, tm=256):
    m = x.shape[0]
    return pl.pallas_call(
        _rmsnorm_kernel,
        grid=(m // tm,),
        in_specs=[pl.BlockSpec((tm, D_MODEL), lambda i: (i, 0)), pl.BlockSpec((1, D_MODEL), lambda i: (0, 0))],
        out_specs=pl.BlockSpec((tm, D_MODEL), lambda i: (i, 0)),
        out_shape=jax.ShapeDtypeStruct((m, D_MODEL), F32),
        compiler_params=pltpu.CompilerParams(
            dimension_semantics=("parallel",),
            vmem_limit_bytes=_vmem_limit(2 * _nbytes((tm, D_MODEL), F32), temp_bytes=_nbytes((tm, D_MODEL), F32))),
        name="final_norm",
    )(x, g.reshape(1, D_MODEL))


def _mm_kernel(x_ref, w_ref, o_ref):
    o_ref[...] = jnp.dot(x_ref[...], w_ref[...], preferred_element_type=F32).astype(o_ref.dtype)


def _mm(x, w, out_dtype, tm, tn, name):
    m, k = x.shape
    n = w.shape[1]
    blocks = _nbytes((tm, k), BF16) + _nbytes((k, tn), BF16) + _nbytes((tm, tn), out_dtype)
    return pl.pallas_call(
        _mm_kernel,
        grid=(m // tm, n // tn),
        in_specs=[pl.BlockSpec((tm, k), lambda i, j: (i, 0)), pl.BlockSpec((k, tn), lambda i, j: (0, j))],
        out_specs=pl.BlockSpec((tm, tn), lambda i, j: (i, j)),
        out_shape=jax.ShapeDtypeStruct((m, n), out_dtype),
        compiler_params=pltpu.CompilerParams(
            dimension_semantics=("parallel", "parallel"),
            vmem_limit_bytes=_vmem_limit(blocks, temp_bytes=_nbytes((tm, tn), F32))),
        name=name,
    )(x, w)


def _merge_kernel(og_ref, uc_ref, wg_ref, wc_ref, mgg_ref, mgc_ref, o_ref):
    y_gla = jnp.dot(og_ref[...], wg_ref[...], preferred_element_type=F32)
    y_conv = jnp.dot(uc_ref[...], wc_ref[...], preferred_element_type=F32)
    mix = _sigmoid(mgg_ref[...].astype(F32)) * y_gla + _sigmoid(mgc_ref[...].astype(F32)) * y_conv
    o_ref[...] = mix.astype(o_ref.dtype)


def _merge(og, uc, proj, w_gla_o, w_conv_o, tm=1024, tn=512):
    m = og.shape[0]
    blocks = (_nbytes((tm, GLA_DV), BF16) + _nbytes((tm, CONV_W), BF16) + _nbytes((GLA_DV, tn), BF16)
              + _nbytes((CONV_W, tn), BF16) + 3 * _nbytes((tm, tn), BF16))
    return pl.pallas_call(
        _merge_kernel,
        grid=(m // tm, D_MODEL // tn),
        in_specs=[
            pl.BlockSpec((tm, GLA_DV), lambda i, j: (i, 0)),
            pl.BlockSpec((tm, CONV_W), lambda i, j: (i, 0)),
            pl.BlockSpec((GLA_DV, tn), lambda i, j: (0, j)),
            pl.BlockSpec((CONV_W, tn), lambda i, j: (0, j)),
            pl.BlockSpec((tm, tn), lambda i, j: (i, COL_MG_GLA // tn + j)),
            pl.BlockSpec((tm, tn), lambda i, j: (i, COL_MG_CONV // tn + j)),
        ],
        out_specs=pl.BlockSpec((tm, tn), lambda i, j: (i, j)),
        out_shape=jax.ShapeDtypeStruct((m, D_MODEL), BF16),
        compiler_params=pltpu.CompilerParams(
            dimension_semantics=("parallel", "parallel"),
            vmem_limit_bytes=_vmem_limit(blocks, temp_bytes=3 * _nbytes((tm, tn), F32))),
        name="merge",
    )(og, uc, w_gla_o, w_conv_o, proj, proj)


def _mm_resid_kernel(x_ref, w_ref, r_ref, gt_ref, o_ref, *acc, nk):
    part = jnp.dot(x_ref[...], w_ref[...], preferred_element_type=F32)
    if nk == 1:
        o_ref[...] = r_ref[...] + gt_ref[...] * part
        return
    acc_ref, = acc
    kk = pl.program_id(2)

    @pl.when(kk == 0)
    def _():
        acc_ref[...] = part

    @pl.when(jnp.logical_and(kk > 0, kk < nk - 1))
    def _():
        acc_ref[...] += part

    @pl.when(kk == nk - 1)
    def _():
        o_ref[...] = r_ref[...] + gt_ref[...] * (acc_ref[...] + part)


def _mm_resid(x, w, resid, mods, which_gate, row_of_tile, tm, tn, tk, name):
    m, k = x.shape
    n = w.shape[1]
    nk = k // tk
    row = lambda i: row_of_tile(i, tm)
    blocks = _nbytes((tm, tk), BF16) + _nbytes((tk, tn), BF16) + 2 * _nbytes((tm, tn), F32)
    return pl.pallas_call(
        functools.partial(_mm_resid_kernel, nk=nk),
        grid=(m // tm, n // tn, nk),
        in_specs=[
            pl.BlockSpec((tm, tk), lambda i, j, kk: (i, kk)),
            pl.BlockSpec((tk, tn), lambda i, j, kk: (kk, j)),
            pl.BlockSpec((tm, tn), lambda i, j, kk: (i, j)),
            _mod_spec(which_gate, tn, row),
        ],
        out_specs=pl.BlockSpec((tm, tn), lambda i, j, kk: (i, j)),
        out_shape=jax.ShapeDtypeStruct((m, n), F32),
        scratch_shapes=[] if nk == 1 else [pltpu.VMEM((tm, tn), F32)],
        compiler_params=pltpu.CompilerParams(
            dimension_semantics=("parallel", "parallel", "arbitrary"),
            vmem_limit_bytes=_vmem_limit(blocks, scratch_bytes=_nbytes((tm, tn), F32),
                                         temp_bytes=_nbytes((tm, tn), F32))),
        name=name,
    )(x, w, resid, mods)


def _ffn_glu_kernel(x_ref, w1_ref, w3_ref, o_ref):
    x = x_ref[...]
    a = jnp.dot(x, w1_ref[...], preferred_element_type=F32)
    b = jnp.dot(x, w3_ref[...], preferred_element_type=F32)
    o_ref[...] = (_silu(a) * b).astype(o_ref.dtype)


def _ffn_glu(x, w1, w3, tm=1024, tn=512):
    m, k = x.shape
    n = w1.shape[1]
    blocks = _nbytes((tm, k), BF16) + 2 * _nbytes((k, tn), BF16) + _nbytes((tm, tn), BF16)
    return pl.pallas_call(
        _ffn_glu_kernel,
        grid=(m // tm, n // tn),
        in_specs=[
            pl.BlockSpec((tm, k), lambda i, j: (i, 0)),
            pl.BlockSpec((k, tn), lambda i, j: (0, j)),
            pl.BlockSpec((k, tn), lambda i, j: (0, j)),
        ],
        out_specs=pl.BlockSpec((tm, tn), lambda i, j: (i, j)),
        out_shape=jax.ShapeDtypeStruct((m, n), BF16),
        compiler_params=pltpu.CompilerParams(
            dimension_semantics=("parallel", "parallel"),
            vmem_limit_bytes=_vmem_limit(blocks, temp_bytes=3 * _nbytes((tm, tn), F32))),
        name="ffn_glu",
    )(x, w1, w3)


def _split_bf16(x):
    hi = x.astype(BF16)
    lo = (x - hi.astype(F32)).astype(BF16)
    return hi, lo


def _gla_group(q_ref, k_ref, v_ref, a_ref, w_a2, b_a, tri, ind, s_ref, o_ref, t0, backward):
    rows = pl.ds(t0, GROUP)
    tn_dims = (((0,), (0,)), ((), ()))
    z = jnp.dot(a_ref[rows, :].astype(BF16), w_a2, preferred_element_type=F32) + b_a
    log_a = (jnp.minimum(z, 0.0) - jnp.log(1.0 + jnp.exp(-jnp.abs(z)))) * (1.0 / GATE_TEMP)
    hi, lo = _split_bf16(log_a)
    yield
    tri_bf = tri.astype(BF16)
    b = jnp.dot(tri_bf, hi, preferred_element_type=F32) + jnp.dot(tri_bf, lo, preferred_element_type=F32)
    tot = (lax.dot_general(hi, ind, tn_dims, preferred_element_type=F32)
           + lax.dot_general(lo, ind, tn_dims, preferred_element_type=F32))
    yield
    decay = jnp.exp(tot)
    ends = []
    for c in range(CHUNKS_PER_GROUP):
        last = c * CHUNK if backward else c * CHUNK + CHUNK - 1
        ends.append(jnp.broadcast_to(b[last:last + 1, :], (CHUNK, GLA_HEAD_K)))
    b_end = jnp.concatenate(ends, axis=0)

    qf = q_ref[rows, :].astype(F32) * (GLA_HEAD_K ** -0.5)
    kf = k_ref[rows, :].astype(F32)
    vb = v_ref[rows, :]
    q_dec = (qf * jnp.exp(b)).astype(BF16)
    k_dec = (kf * jnp.exp(-b)).astype(BF16)
    k_end = (kf * jnp.exp(b_end - b)).astype(BF16)
    yield
    scores = lax.dot_general(q_dec, k_dec, (((1,), (1,)), ((), ())), preferred_element_type=F32)
    scores = jnp.where(tri, scores, 0.0).astype(BF16)
    o_intra = jnp.dot(scores, vb, preferred_element_type=F32)
    yield

    order = range(CHUNKS_PER_GROUP - 1, -1, -1) if backward else range(CHUNKS_PER_GROUP)
    for c in order:
        cr = slice(c * CHUNK, (c + 1) * CHUNK)
        s = s_ref[...]
        o_ref[pl.ds(t0 + c * CHUNK, CHUNK), :] = o_intra[cr, :] + jnp.dot(
            q_dec[cr, :], s.astype(BF16), preferred_element_type=F32)
        s_ref[...] = decay[:, c:c + 1] * s + lax.dot_general(k_end[cr, :], vb[cr, :], tn_dims,
                                                             preferred_element_type=F32)
        yield


def _gla_kernel(*refs, seq_len, has_init, emit_state):
    (q_ref, k_ref, v_ref, g_ref, a_ref, waf_ref, baf_ref, wab_ref, bab_ref, gn_ref), refs = refs[:10], refs[10:]
    if has_init:
        (s0f_ref, s0b_ref), refs = refs[:2], refs[2:]
    og_ref, refs = refs[0], refs[1:]
    if emit_state:
        (sf_out_ref, sb_out_ref), refs = refs[:2], refs[2:]
    of_ref, ob_ref, sf_ref, sb_ref = refs
    n_groups = seq_len // GROUP

    if has_init:
        sf_ref[...] = s0f_ref[...]
        sb_ref[...] = s0b_ref[...]
    else:
        sf_ref[...] = jnp.zeros_like(sf_ref)
        sb_ref[...] = jnp.zeros_like(sb_ref)

    row = lax.broadcasted_iota(jnp.int32, (GROUP, GROUP), 0)
    col = lax.broadcasted_iota(jnp.int32, (GROUP, GROUP), 1)
    same_chunk = (row // CHUNK) == (col // CHUNK)
    tri_f = jnp.logical_and(same_chunk, row >= col)
    tri_b = jnp.logical_and(same_chunk, col >= row)
    tok = lax.broadcasted_iota(jnp.int32, (GROUP, LANES), 0)
    lane = lax.broadcasted_iota(jnp.int32, (GROUP, LANES), 1)
    ind = ((tok // CHUNK) == lane).astype(BF16)

    def scan_step(i, carry):
        tf = pl.multiple_of(i * GROUP, GROUP)
        tb = pl.multiple_of((n_groups - 1 - i) * GROUP, GROUP)
        directions = [
            _gla_group(q_ref, k_ref, v_ref, a_ref, waf_ref[...], baf_ref[...], tri_f, ind, sf_ref, of_ref, tf,
                       backward=False),
            _gla_group(q_ref, k_ref, v_ref, a_ref, wab_ref[...], bab_ref[...], tri_b, ind, sb_ref, ob_ref, tb,
                       backward=True),
        ]
        while directions:
            directions = [d for d in directions if next(d, True) is None]
        return carry

    if n_groups == 1:
        scan_step(0, 0)
    else:
        lax.fori_loop(0, n_groups, scan_step, 0)

    def out_step(i, carry):
        t = pl.ds(pl.multiple_of(i * CHUNK, CHUNK), CHUNK)
        o = of_ref[t, :] + ob_ref[t, :]
        ms = jnp.mean(o * o, axis=-1, keepdims=True)
        o = o * lax.rsqrt(ms + EPS) * gn_ref[...]
        og_ref[t, :] = (o * _silu(g_ref[t, :].astype(F32))).astype(og_ref.dtype)
        return carry

    lax.fori_loop(0, seq_len // CHUNK, out_step, 0)

    if emit_state:
        sf_out_ref[...] = sf_ref[...]
        sb_out_ref[...] = sb_ref[...]


def _gla(proj, a_lr, w_a2f, b_af, w_a2b, b_ab, gn, batch, seq_len, s0=None, emit_state=False):
    proj3 = proj.reshape(batch, seq_len, HALF_WIDTH)
    a3 = a_lr.reshape(batch, seq_len, LANES)
    kq, kv = COL_K // GLA_HEAD_K, COL_V // GLA_HEAD_V
    kg = COL_G // GLA_HEAD_V
    in_specs = [
        pl.BlockSpec((None, seq_len, GLA_HEAD_K), lambda b, h: (b, 0, h)),
        pl.BlockSpec((None, seq_len, GLA_HEAD_K), lambda b, h: (b, 0, kq + h)),
        pl.BlockSpec((None, seq_len, GLA_HEAD_V), lambda b, h: (b, 0, kv + h)),
        pl.BlockSpec((None, seq_len, GLA_HEAD_V), lambda b, h: (b, 0, kg + h)),
        pl.BlockSpec((None, seq_len, LANES), lambda b, h: (b, 0, 0)),
        pl.BlockSpec((LANES, GLA_HEAD_K), lambda b, h: (0, h)),
        pl.BlockSpec((1, GLA_HEAD_K), lambda b, h: (0, h)),
        pl.BlockSpec((LANES, GLA_HEAD_K), lambda b, h: (0, h)),
        pl.BlockSpec((1, GLA_HEAD_K), lambda b, h: (0, h)),
        pl.BlockSpec((1, GLA_HEAD_V), lambda b, h: (0, 0)),
    ]
    args = [proj3, proj3, proj3, proj3, a3, w_a2f, b_af, w_a2b, b_ab, gn]
    state_spec = pl.BlockSpec((None, None, GLA_HEAD_K, GLA_HEAD_V), lambda b, h: (b, h, 0, 0))
    if s0 is not None:
        in_specs += [state_spec, state_spec]
        args += list(s0)
    out_specs = [pl.BlockSpec((None, seq_len, GLA_HEAD_V), lambda b, h: (b, 0, h))]
    out_shape = [jax.ShapeDtypeStruct((batch, seq_len, GLA_DV), BF16)]
    if emit_state:
        out_specs += [state_spec, state_spec]
        out_shape += [jax.ShapeDtypeStruct((batch, GLA_HEADS, GLA_HEAD_K, GLA_HEAD_V), F32)] * 2
    state_bytes = _nbytes((GLA_HEAD_K, GLA_HEAD_V), F32)
    blocks = (2 * _nbytes((seq_len, GLA_HEAD_K), BF16) + 3 * _nbytes((seq_len, GLA_HEAD_V), BF16)
              + _nbytes((seq_len, LANES), F32) + 4 * state_bytes)
    scratch = 2 * _nbytes((seq_len, GLA_HEAD_V), F32) + 2 * state_bytes
    outs = pl.pallas_call(
        functools.partial(_gla_kernel, seq_len=seq_len, has_init=s0 is not None, emit_state=emit_state),
        grid=(batch, GLA_HEADS),
        in_specs=in_specs,
        out_specs=out_specs,
        out_shape=out_shape,
        scratch_shapes=[
            pltpu.VMEM((seq_len, GLA_HEAD_V), F32),
            pltpu.VMEM((seq_len, GLA_HEAD_V), F32),
            pltpu.VMEM((GLA_HEAD_K, GLA_HEAD_V), F32),
            pltpu.VMEM((GLA_HEAD_K, GLA_HEAD_V), F32),
        ],
        compiler_params=pltpu.CompilerParams(
            dimension_semantics=("parallel", "arbitrary"),
            vmem_limit_bytes=_vmem_limit(blocks, scratch_bytes=scratch, temp_bytes=8 << 20)),
        name="gla",
    )(*args)
    og = outs[0].reshape(batch * seq_len, GLA_DV)
    return (og, outs[1], outs[2]) if emit_state else (og, None, None)


def _conv_kernel(ua_ref, ug_ref, cw_ref, cb_ref, lng_ref, lnb_ref, o_ref, pad_ref, shift_ref, conv_ref, *, seg_len):
    n_seg = CONV_TILE // seg_len
    stride = seg_len + 2 * CONV_HALO
    pad_rows = n_seg * stride
    rows = min(seg_len, 64)
    glu = ua_ref[...].astype(F32) * _sigmoid(ug_ref[...].astype(F32))
    halo = jnp.zeros((CONV_HALO, CONV_W), F32)
    for s in range(n_seg):
        base = s * stride
        pad_ref[base:base + CONV_HALO, :] = halo
        pad_ref[base + CONV_HALO:base + CONV_HALO + seg_len, :] = glu[s * seg_len:(s + 1) * seg_len, :]
        pad_ref[base + CONV_HALO + seg_len:base + stride, :] = halo
    pad_ref[pad_rows:pad_rows + SUBLANES, :] = jnp.zeros((SUBLANES, CONV_W), F32)
    for r in range(1, SUBLANES):
        shift_ref[r - 1, :, :] = pad_ref[r:pad_rows + r, :]

    def lane_block(cb, carry):
        lanes = pl.ds(pl.multiple_of(cb * LANES, LANES), LANES)
        bias = cb_ref[:, lanes]
        for s in range(n_seg):
            for rb in range(seg_len // rows):
                first = s * stride + CONV_HALO - CONV_K // 2 + rb * rows
                acc = jnp.zeros((rows, LANES), F32)
                for tap in range(CONV_K):
                    aligned, r = divmod(first + tap, SUBLANES)
                    at = pl.ds(aligned * SUBLANES, rows)
                    window = pad_ref[at, lanes] if r == 0 else shift_ref[r - 1, at, lanes]
                    acc = acc + window * cw_ref[tap:tap + 1, lanes]
                conv_ref[pl.ds(s * seg_len + rb * rows, rows), lanes] = acc + bias
        return carry

    lax.fori_loop(0, CONV_W // LANES, lane_block, 0)

    u = conv_ref[...]
    mu = jnp.mean(u, axis=-1, keepdims=True)
    uc = u - mu
    var = jnp.mean(uc * uc, axis=-1, keepdims=True)
    y = uc * lax.rsqrt(var + EPS) * lng_ref[...] + lnb_ref[...]
    o_ref[...] = _silu(y).astype(o_ref.dtype)


def _conv_module(proj, conv_w, conv_b, ln_g, ln_b, seg_len):
    m = proj.shape[0]
    n_seg = CONV_TILE // seg_len
    pad_rows = n_seg * (seg_len + 2 * CONV_HALO)
    row = lambda v: v.reshape(1, CONV_W)
    blocks = 3 * _nbytes((CONV_TILE, CONV_W), BF16)
    scratch = SUBLANES * _nbytes((pad_rows + SUBLANES, CONV_W), F32) + _nbytes((CONV_TILE, CONV_W), F32)
    return pl.pallas_call(
        functools.partial(_conv_kernel, seg_len=seg_len),
        grid=(m // CONV_TILE,),
        in_specs=[
            pl.BlockSpec((CONV_TILE, CONV_W), lambda i: (i, COL_UA // CONV_W)),
            pl.BlockSpec((CONV_TILE, CONV_W), lambda i: (i, COL_UG // CONV_W)),
            pl.BlockSpec((CONV_K, CONV_W), lambda i: (0, 0)),
            pl.BlockSpec((1, CONV_W), lambda i: (0, 0)),
            pl.BlockSpec((1, CONV_W), lambda i: (0, 0)),
            pl.BlockSpec((1, CONV_W), lambda i: (0, 0)),
        ],
        out_specs=pl.BlockSpec((CONV_TILE, CONV_W), lambda i: (i, 0)),
        out_shape=jax.ShapeDtypeStruct((m, CONV_W), BF16),
        scratch_shapes=[pltpu.VMEM((pad_rows + SUBLANES, CONV_W), F32),
                        pltpu.VMEM((SUBLANES - 1, pad_rows, CONV_W), F32),
                        pltpu.VMEM((CONV_TILE, CONV_W), F32)],
        compiler_params=pltpu.CompilerParams(
            dimension_semantics=("parallel",),
            vmem_limit_bytes=_vmem_limit(blocks, scratch_bytes=scratch,
                                         temp_bytes=4 * _nbytes((CONV_TILE, CONV_W), F32))),
        name="conv_module",
    )(proj, proj, conv_w, row(conv_b), row(ln_g), row(ln_b))


def _trunk(x, mods, row_of_tile, batch, seq_len, conv_seg, wts, s0, emit_state):
    h = _norm_mod(x, wts["norm_mix_g"], mods, 0, 1, row_of_tile)
    proj_att = _mm(h, wts["w_att"], BF16, 1024, 1024, "in_proj_att")
    proj_mix = _mm(h, wts["w_mix"], BF16, 1024, 1024, "in_proj_mix")
    a_lr = _mm(h, wts["w_decay"], F32, 1024, LANES, "decay_proj")
    og, s_f, s_b = _gla(proj_att, a_lr, wts["w_a2f"], wts["b_af"], wts["w_a2b"], wts["b_ab"], wts["gla_norm_g"],
                        batch, seq_len, s0=s0, emit_state=emit_state)
    uc = _conv_module(proj_mix, wts["conv_w"], wts["conv_b"], wts["conv_ln_g"], wts["conv_ln_b"], conv_seg)
    mix = _merge(og, uc, proj_mix, wts["w_gla_o"], wts["w_conv_o"])
    x1 = _mm_resid(mix, wts["w_out"], x, mods, 2, row_of_tile, 1024, 1024, D_MODEL, "out_proj")
    h2 = _norm_mod(x1, wts["norm_ffn_g"], mods, 3, 4, row_of_tile)
    hid = _ffn_glu(h2, wts["w_ffn1"], wts["w_ffn3"])
    x2 = _mm_resid(hid, wts["w_ffn2"], x1, mods, 5, row_of_tile, 1024, 1024, FFN_PAD // 4, "ffn_down")
    return _final_norm(x2, wts["final_norm_g"]), s_f, s_b


def kernel(x_prompt, x_sample, state_gla_fwd, state_gla_bwd, c, c_ctx, norm_mix_g, w_ada, b_ada, w_in, w_a2_fwd, b_a_fwd, w_a2_bwd, b_a_bwd, gla_norm_g, w_gla_o, conv_w, conv_b, conv_ln_g, conv_ln_b, w_conv_o, w_out, norm_ffn_g, w_ffn1, w_ffn3, w_ffn2, final_norm_g):
    n_req, seq, _ = x_prompt.shape
    dec_batch, dec_seq, _ = x_sample.shape
    depth = w_in.shape[0]
    assert depth == 1 and dec_batch + 1 <= MOD_ROWS

    wi = w_in[0]
    pad_ffn = FFN_PAD - FFN_HIDDEN
    decay_rows = lambda w, first: jnp.pad(w.astype(BF16), ((first, LANES - GATE_RANK - first), (0, 0)))
    wts = {
        "w_att": wi[:, :DECAY_COL0].astype(BF16),
        "w_mix": wi[:, DECAY_COL0 + 2 * GATE_RANK:].astype(BF16),
        "w_decay": jnp.pad(wi[:, DECAY_COL0:DECAY_COL0 + 2 * GATE_RANK].astype(BF16),
                           ((0, 0), (0, LANES - 2 * GATE_RANK))),
        "w_a2f": decay_rows(w_a2_fwd[0], 0),
        "w_a2b": decay_rows(w_a2_bwd[0], GATE_RANK),
        "b_af": b_a_fwd[0].reshape(1, GLA_DK),
        "b_ab": b_a_bwd[0].reshape(1, GLA_DK),
        "gla_norm_g": gla_norm_g[0].reshape(1, GLA_HEAD_V),
        "w_gla_o": w_gla_o[0].astype(BF16),
        "conv_w": conv_w[0], "conv_b": conv_b[0], "conv_ln_g": conv_ln_g[0], "conv_ln_b": conv_ln_b[0],
        "w_conv_o": w_conv_o[0].astype(BF16),
        "w_out": w_out[0].astype(BF16),
        "w_ffn1": _cast_pad(w_ffn1[0], (D_MODEL, FFN_PAD), (D_MODEL, pad_ffn)),
        "w_ffn3": _cast_pad(w_ffn3[0], (D_MODEL, FFN_PAD), (D_MODEL, pad_ffn)),
        "w_ffn2": _cast_pad(w_ffn2[0], (FFN_PAD, D_MODEL), (pad_ffn, D_MODEL)),
        "norm_mix_g": norm_mix_g[0], "norm_ffn_g": norm_ffn_g[0], "final_norm_g": final_norm_g,
    }

    cond = jnp.concatenate([c_ctx[None, :], c, jnp.zeros((MOD_ROWS - 1 - dec_batch, D_MODEL), F32)], axis=0)
    mods = _ada(cond, w_ada[0], b_ada[0]).reshape(MOD_ROWS, 1, N_MOD * D_MODEL)

    ctx_row = lambda i, tm: 0
    lat_row = lambda i, tm: 1 + (i * tm) // dec_seq

    y_p, s_f, s_b = _trunk(x_prompt.reshape(n_req * seq, D_MODEL), mods, ctx_row, n_req, seq, seq, wts,
                           None, True)
    y_s, _, _ = _trunk(x_sample.reshape(dec_batch * dec_seq, D_MODEL), mods, lat_row, dec_batch, dec_seq, GRID_W,
                       wts, (state_gla_fwd[:, 0], state_gla_bwd[:, 0]), False)
    state_shape = (n_req, depth, GLA_HEADS, GLA_HEAD_K, GLA_HEAD_V)
    return (y_p.reshape(n_req, seq, D_MODEL), y_s.reshape(dec_batch, dec_seq, D_MODEL),
            s_f.reshape(state_shape), s_b.reshape(state_shape))
```

```python
import functools

import jax
import jax.numpy as jnp
from jax import lax
from jax.experimental import pallas as pl
from jax.experimental.pallas import tpu as pltpu

F32 = jnp.float32
BF16 = jnp.bfloat16

D_MODEL = 4096
GLA_HEADS = 8
GLA_HEAD_K = 256
GLA_HEAD_V = 512
GLA_DK = GLA_HEADS * GLA_HEAD_K
GLA_DV = GLA_HEADS * GLA_HEAD_V
GATE_RANK = 16
GATE_TEMP = 16.0
CHUNK = 64
GRID_W = 64
CONV_W = D_MODEL // 2
CONV_K = 31
FFN_HIDDEN = 11008
N_MOD = 6
EPS = 1e-6

LANES = 128
SUBLANES = 8
VMEM_BYTES_V7X = 64 * 1024 * 1024

COL_Q = 0
COL_K = GLA_DK
COL_V = 2 * GLA_DK
COL_G = COL_V + GLA_DV
HALF_WIDTH = COL_G + GLA_DV
COL_UA = 0
COL_UG = CONV_W
COL_MG_GLA = 2 * CONV_W
COL_MG_CONV = COL_MG_GLA + D_MODEL
assert COL_MG_CONV + D_MODEL == HALF_WIDTH
DECAY_COL0 = HALF_WIDTH
GROUP = 256
CHUNKS_PER_GROUP = GROUP // CHUNK
FFN_PAD = 11264
MOD_ROWS = 8
CONV_TILE = 256
CONV_HALO = 16


def _vmem_limit(block_bytes, scratch_bytes=0, temp_bytes=0):
    return int(min(VMEM_BYTES_V7X - (2 << 20), 2 * block_bytes + scratch_bytes + temp_bytes + (4 << 20)))


def _nbytes(shape, dtype):
    n = 1
    for s in shape:
        n *= s
    return n * jnp.dtype(dtype).itemsize


def _sigmoid(x):
    return jax.nn.sigmoid(x)


def _silu(x):
    return x * jax.nn.sigmoid(x)


def _cast_pad_kernel(x_ref, o_ref, *, valid_blocks):
    inside = jnp.logical_and(pl.program_id(0) < valid_blocks[0], pl.program_id(1) < valid_blocks[1])

    @pl.when(inside)
    def _():
        o_ref[...] = x_ref[...].astype(o_ref.dtype)

    @pl.when(jnp.logical_not(inside))
    def _():
        o_ref[...] = jnp.zeros_like(o_ref)


def _cast_pad(w, out_shape, block):
    assert all(s % b == 0 and o % b == 0 for s, o, b in zip(w.shape, out_shape, block))
    valid = (w.shape[0] // block[0], w.shape[1] // block[1])
    return pl.pallas_call(
        functools.partial(_cast_pad_kernel, valid_blocks=valid),
        grid=(out_shape[0] // block[0], out_shape[1] // block[1]),
        in_specs=[pl.BlockSpec(block, lambda i, j: (jnp.minimum(i, valid[0] - 1), jnp.minimum(j, valid[1] - 1)))],
        out_specs=pl.BlockSpec(block, lambda i, j: (i, j)),
        out_shape=jax.ShapeDtypeStruct(out_shape, BF16),
        compiler_params=pltpu.CompilerParams(
            dimension_semantics=("parallel", "parallel"),
            vmem_limit_bytes=_vmem_limit(_nbytes(block, F32) + _nbytes(block, BF16))),
        name="cast_pad",
    )(w)


def _split_cast_kernel(w_ref, att_ref, mix_ref, dec_ref):
    att_ref[...] = w_ref[:, :HALF_WIDTH].astype(BF16)
    mix_ref[...] = w_ref[:, DECAY_COL0 + 2 * GATE_RANK:].astype(BF16)
    dec = w_ref[:, DECAY_COL0:DECAY_COL0 + LANES]
    lane = lax.broadcasted_iota(jnp.int32, dec.shape, 1)
    dec_ref[...] = jnp.where(lane < 2 * GATE_RANK, dec, 0.0).astype(BF16)


def _split_cast(w_in, tr=128):
    k, width = w_in.shape
    assert width == 2 * HALF_WIDTH + 2 * GATE_RANK
    half = lambda: pl.BlockSpec((tr, HALF_WIDTH), lambda i: (i, 0))
    return pl.pallas_call(
        _split_cast_kernel,
        grid=(k // tr,),
        in_specs=[pl.BlockSpec((tr, width), lambda i: (i, 0))],
        out_specs=[half(), half(), pl.BlockSpec((tr, LANES), lambda i: (i, 0))],
        out_shape=[jax.ShapeDtypeStruct((k, HALF_WIDTH), BF16), jax.ShapeDtypeStruct((k, HALF_WIDTH), BF16),
                   jax.ShapeDtypeStruct((k, LANES), BF16)],
        compiler_params=pltpu.CompilerParams(
            dimension_semantics=("parallel",),
            vmem_limit_bytes=_vmem_limit(_nbytes((tr, width), F32) + 2 * _nbytes((tr, HALF_WIDTH), BF16),
                                         temp_bytes=_nbytes((tr, width), F32))),
        name="split_cast",
    )(w_in)


def _ada_kernel(c_ref, w_ref, b_ref, o_ref):
    s = _silu(c_ref[...]).astype(BF16)
    o_ref[...] = jnp.dot(s, w_ref[...].astype(BF16), preferred_element_type=F32) + b_ref[...]


def _ada(cond, w_ada, b_ada):
    n = w_ada.shape[1]
    tn = 512
    return pl.pallas_call(
        _ada_kernel,
        grid=(n // tn,),
        in_specs=[
            pl.BlockSpec((MOD_ROWS, D_MODEL), lambda j: (0, 0)),
            pl.BlockSpec((D_MODEL, tn), lambda j: (0, j)),
            pl.BlockSpec((1, tn), lambda j: (0, j)),
        ],
        out_specs=pl.BlockSpec((MOD_ROWS, tn), lambda j: (0, j)),
        out_shape=jax.ShapeDtypeStruct((MOD_ROWS, n), F32),
        compiler_params=pltpu.CompilerParams(
            dimension_semantics=("arbitrary",),
            vmem_limit_bytes=_vmem_limit(_nbytes((D_MODEL, tn), F32), temp_bytes=_nbytes((D_MODEL, tn), BF16))),
        name="ada",
    )(cond, w_ada, b_ada.reshape(1, n))


def _norm_mod_kernel(x_ref, g_ref, sc_ref, sh_ref, o_ref):
    x = x_ref[...]
    ms = jnp.mean(x * x, axis=-1, keepdims=True)
    h = x * lax.rsqrt(ms + EPS) * g_ref[...]
    o_ref[...] = (h * (1.0 + sc_ref[...]) + sh_ref[...]).astype(o_ref.dtype)


def _rmsnorm_kernel(x_ref, g_ref, o_ref):
    x = x_ref[...]
    ms = jnp.mean(x * x, axis=-1, keepdims=True)
    o_ref[...] = x * lax.rsqrt(ms + EPS) * g_ref[...]


def _mod_spec(which, tn, row_of_tile):
    nb = D_MODEL // tn
    return pl.BlockSpec((None, 1, tn), lambda i, j, *_: (row_of_tile(i), 0, which * nb + j))


def _norm_mod(x, g, mods, which_shift, which_scale, row_of_tile, tm=256):
    m = x.shape[0]
    row = lambda i: row_of_tile(i, tm)
    return pl.pallas_call(
        _norm_mod_kernel,
        grid=(m // tm, 1),
        in_specs=[
            pl.BlockSpec((tm, D_MODEL), lambda i, j: (i, 0)),
            pl.BlockSpec((1, D_MODEL), lambda i, j: (0, 0)),
            _mod_spec(which_scale, D_MODEL, row),
            _mod_spec(which_shift, D_MODEL, row),
        ],
        out_specs=pl.BlockSpec((tm, D_MODEL), lambda i, j: (i, 0)),
        out_shape=jax.ShapeDtypeStruct((m, D_MODEL), BF16),
        compiler_params=pltpu.CompilerParams(
            dimension_semantics=("parallel", "arbitrary"),
            vmem_limit_bytes=_vmem_limit(_nbytes((tm, D_MODEL), F32) + _nbytes((tm, D_MODEL), BF16),
                                         temp_bytes=2 * _nbytes((tm, D_MODEL), F32))),
        name="norm_mod",
    )(x, g.reshape(1, D_MODEL), mods, mods)


def _final_norm(x, g, tm=256):
    m = x.shape[0]
    return pl.pallas_call(
        _rmsnorm_kernel,
        grid=(m // tm,),
        in_specs=[pl.BlockSpec((tm, D_MODEL), lambda i: (i, 0)), pl.BlockSpec((1, D_MODEL), lambda i: (0, 0))],
        out_specs=pl.BlockSpec((tm, D_MODEL), lambda i: (i, 0)),
        out_shape=jax.ShapeDtypeStruct((m, D_MODEL), F32),
        compiler_params=pltpu.CompilerParams(
            dimension_semantics=("parallel",),
            vmem_limit_bytes=_vmem_limit(2 * _nbytes((tm, D_MODEL), F32), temp_bytes=_nbytes((tm, D_MODEL), F32))),
        name="final_norm",
    )(x, g.reshape(1, D_MODEL))


def _mm_kernel(x_ref, w_ref, o_ref):
    o_ref[...] = jnp.dot(x_ref[...], w_ref[...], preferred_element_type=F32).astype(o_ref.dtype)


def _mm(x, w, out_dtype, tm, tn, name):
    m, k = x.shape
    n = w.shape[1]
    blocks = _nbytes((tm, k), BF16) + _nbytes((k, tn), BF16) + _nbytes((tm, tn), out_dtype)
    return pl.pallas_call(
        _mm_kernel,
        grid=(m // tm, n // tn),
        in_specs=[pl.BlockSpec((tm, k), lambda i, j: (i, 0)), pl.BlockSpec((k, tn), lambda i, j: (0, j))],
        out_specs=pl.BlockSpec((tm, tn), lambda i, j: (i, j)),
        out_shape=jax.ShapeDtypeStruct((m, n), out_dtype),
        compiler_params=pltpu.CompilerParams(
            dimension_semantics=("parallel", "parallel"),
            vmem_limit_bytes=_vmem_limit(blocks, temp_bytes=_nbytes((tm, tn), F32))),
        name=name,
    )(x, w)


def _merge_kernel(og_ref, uc_ref, wg_ref, wc_ref, mgg_ref, mgc_ref, o_ref):
    y_gla = jnp.dot(og_ref[...], wg_ref[...], preferred_element_type=F32)
    y_conv = jnp.dot(uc_ref[...], wc_ref[...], preferred_element_type=F32)
    mix = _sigmoid(mgg_ref[...].astype(F32)) * y_gla + _sigmoid(mgc_ref[...].astype(F32)) * y_conv
    o_ref[...] = mix.astype(o_ref.dtype)


def _merge(og, uc, proj, w_gla_o, w_conv_o, tm=1024, tn=512):
    m = og.shape[0]
    blocks = (_nbytes((tm, GLA_DV), BF16) + _nbytes((tm, CONV_W), BF16) + _nbytes((GLA_DV, tn), BF16)
              + _nbytes((CONV_W, tn), BF16) + 3 * _nbytes((tm, tn), BF16))
    return pl.pallas_call(
        _merge_kernel,
        grid=(m // tm, D_MODEL // tn),
        in_specs=[
            pl.BlockSpec((tm, GLA_DV), lambda i, j: (i, 0)),
            pl.BlockSpec((tm, CONV_W), lambda i, j: (i, 0)),
            pl.BlockSpec((GLA_DV, tn), lambda i, j: (0, j)),
            pl.BlockSpec((CONV_W, tn), lambda i, j: (0, j)),
            pl.BlockSpec((tm, tn), lambda i, j: (i, COL_MG_GLA // tn + j)),
            pl.BlockSpec((tm, tn), lambda i, j: (i, COL_MG_CONV // tn + j)),
        ],
        out_specs=pl.BlockSpec((tm, tn), lambda i, j: (i, j)),
        out_shape=jax.ShapeDtypeStruct((m, D_MODEL), BF16),
        compiler_params=pltpu.CompilerParams(
            dimension_semantics=("parallel", "parallel"),
            vmem_limit_bytes=_vmem_limit(blocks, temp_bytes=3 * _nbytes((tm, tn), F32))),
        name="merge",
    )(og, uc, w_gla_o, w_conv_o, proj, proj)


def _mm_resid_kernel(x_ref, w_ref, r_ref, gt_ref, o_ref, *acc, nk):
    part = jnp.dot(x_ref[...], w_ref[...], preferred_element_type=F32)
    if nk == 1:
        o_ref[...] = r_ref[...] + gt_ref[...] * part
        return
    acc_ref, = acc
    kk = pl.program_id(2)

    @pl.when(kk == 0)
    def _():
        acc_ref[...] = part

    @pl.when(jnp.logical_and(kk > 0, kk < nk - 1))
    def _():
        acc_ref[...] += part

    @pl.when(kk == nk - 1)
    def _():
        o_ref[...] = r_ref[...] + gt_ref[...] * (acc_ref[...] + part)


def _mm_resid(x, w, resid, mods, which_gate, row_of_tile, tm, tn, tk, name):
    m, k = x.shape
    n = w.shape[1]
    nk = k // tk
    row = lambda i: row_of_tile(i, tm)
    blocks = _nbytes((tm, tk), BF16) + _nbytes((tk, tn), BF16) + 2 * _nbytes((tm, tn), F32)
    return pl.pallas_call(
        functools.partial(_mm_resid_kernel, nk=nk),
        grid=(m // tm, n // tn, nk),
        in_specs=[
            pl.BlockSpec((tm, tk), lambda i, j, kk: (i, kk)),
            pl.BlockSpec((tk, tn), lambda i, j, kk: (kk, j)),
            pl.BlockSpec((tm, tn), lambda i, j, kk: (i, j)),
            _mod_spec(which_gate, tn, row),
        ],
        out_specs=pl.BlockSpec((tm, tn), lambda i, j, kk: (i, j)),
        out_shape=jax.ShapeDtypeStruct((m, n), F32),
        scratch_shapes=[] if nk == 1 else [pltpu.VMEM((tm, tn), F32)],
        compiler_params=pltpu.CompilerParams(
            dimension_semantics=("parallel", "parallel", "arbitrary"),
            vmem_limit_bytes=_vmem_limit(blocks, scratch_bytes=_nbytes((tm, tn), F32),
                                         temp_bytes=_nbytes((tm, tn), F32))),
        name=name,
    )(x, w, resid, mods)


def _ffn_glu_kernel(x_ref, w1_ref, w3_ref, o_ref):
    x = x_ref[...]
    a = jnp.dot(x, w1_ref[...], preferred_element_type=F32)
    b = jnp.dot(x, w3_ref[...], preferred_element_type=F32)
    o_ref[...] = (_silu(a) * b).astype(o_ref.dtype)


def _ffn_glu(x, w1, w3, tm=1024, tn=512):
    m, k = x.shape
    n = w1.shape[1]
    blocks = _nbytes((tm, k), BF16) + 2 * _nbytes((k, tn), BF16) + _nbytes((tm, tn), BF16)
    return pl.pallas_call(
        _ffn_glu_kernel,
        grid=(m // tm, n // tn),
        in_specs=[
            pl.BlockSpec((tm, k), lambda i, j: (i, 0)),
            pl.BlockSpec((k, tn), lambda i, j: (0, j)),
            pl.BlockSpec((k, tn), lambda i, j: (0, j)),
        ],
        out_specs=pl.BlockSpec((tm, tn), lambda i, j: (i, j)),
        out_shape=jax.ShapeDtypeStruct((m, n), BF16),
        compiler_params=pltpu.CompilerParams(
            dimension_semantics=("parallel", "parallel"),
            vmem_limit_bytes=_vmem_limit(blocks, temp_bytes=3 * _nbytes((tm, tn), F32))),
        name="ffn_glu",
    )(x, w1, w3)


def _split_bf16(x):
    hi = x.astype(BF16)
    lo = (x - hi.astype(F32)).astype(BF16)
    return hi, lo


def _gla_group(q_ref, k_ref, v_ref, a_ref, w_a2, b_a, tri, cross, s_ref, o_ref, t0, backward):
    rows = pl.ds(t0, GROUP)
    tn_dims = (((0,), (0,)), ((), ()))
    z = jnp.dot(a_ref[rows, :].astype(BF16), w_a2, preferred_element_type=F32) + b_a
    log_a = (jnp.minimum(z, 0.0) - jnp.log(1.0 + jnp.exp(-jnp.abs(z)))) * (1.0 / GATE_TEMP)
    hi, lo = _split_bf16(log_a)
    yield
    tri_bf = tri.astype(BF16)
    b = jnp.dot(tri_bf, hi, preferred_element_type=F32) + jnp.dot(tri_bf, lo, preferred_element_type=F32)
    yield
    nt_dims = (((1,), (1,)), ((), ()))
    totals = [b[c * CHUNK:c * CHUNK + 1, :] if backward else b[(c + 1) * CHUNK - 1:(c + 1) * CHUNK, :]
              for c in range(CHUNKS_PER_GROUP)]
    decays = [jnp.exp(t) for t in totals]
    scanned_first = lambda c: (c % 2 == 1) if backward else (c % 2 == 0)
    over_chunk = lambda r: jnp.broadcast_to(r, (CHUNK, GLA_HEAD_K))
    one = jnp.ones((CHUNK, GLA_HEAD_K), F32)
    chunks = range(CHUNKS_PER_GROUP)
    b_end = jnp.concatenate([over_chunk(totals[c]) for c in chunks], axis=0)
    q_scale = jnp.concatenate([one if scanned_first(c) else over_chunk(decays[c ^ 1]) for c in chunks], axis=0)
    k_scale = jnp.concatenate([over_chunk(decays[c ^ 1]) if scanned_first(c) else one for c in chunks], axis=0)
    tot_cols = jnp.concatenate(totals + [jnp.zeros((LANES - CHUNKS_PER_GROUP, GLA_HEAD_K), F32)], axis=0).T

    qf = q_ref[rows, :].astype(F32) * (GLA_HEAD_K ** -0.5)
    kf = k_ref[rows, :].astype(F32)
    vb = v_ref[rows, :]
    q_f32 = qf * jnp.exp(b)
    k_f32 = kf * jnp.exp(b_end - b)
    q_dec = q_f32.astype(BF16)
    q_pair = (q_f32 * q_scale).astype(BF16)
    k_dec = (kf * jnp.exp(-b)).astype(BF16)
    k_end = k_f32.astype(BF16)
    k_pair = (k_f32 * k_scale).astype(BF16)
    yield
    within = lax.dot_general(q_dec, k_dec, nt_dims, preferred_element_type=F32)
    across = lax.dot_general(q_dec, k_end, nt_dims, preferred_element_type=F32)
    scores = jnp.where(tri, within, jnp.where(cross, across, 0.0)).astype(BF16)
    o_intra = jnp.dot(scores, vb, preferred_element_type=F32)
    yield

    n_pairs = CHUNKS_PER_GROUP // 2
    for p in (range(n_pairs - 1, -1, -1) if backward else range(n_pairs)):
        pr = slice(2 * p * CHUNK, 2 * (p + 1) * CHUNK)
        decay = jnp.exp(tot_cols[:, 2 * p:2 * p + 1] + tot_cols[:, 2 * p + 1:2 * p + 2])
        s = s_ref[...]
        o_ref[pl.ds(t0 + 2 * p * CHUNK, 2 * CHUNK), :] = o_intra[pr, :] + jnp.dot(
            q_pair[pr, :], s.astype(BF16), preferred_element_type=F32)
        s_ref[...] = decay * s + lax.dot_general(k_pair[pr, :], vb[pr, :], tn_dims, preferred_element_type=F32)
        yield


def _gla_kernel(*refs, seq_len, has_init, emit_state):
    (q_ref, k_ref, v_ref, g_ref, a_ref, waf_ref, baf_ref, wab_ref, bab_ref, gn_ref), refs = refs[:10], refs[10:]
    if has_init:
        (s0f_ref, s0b_ref), refs = refs[:2], refs[2:]
    og_ref, refs = refs[0], refs[1:]
    if emit_state:
        (sf_out_ref, sb_out_ref), refs = refs[:2], refs[2:]
    of_ref, ob_ref, sf_ref, sb_ref = refs
    n_groups = seq_len // GROUP

    if has_init:
        sf_ref[...] = s0f_ref[...]
        sb_ref[...] = s0b_ref[...]
    else:
        sf_ref[...] = jnp.zeros_like(sf_ref)
        sb_ref[...] = jnp.zeros_like(sb_ref)

    row = lax.broadcasted_iota(jnp.int32, (GROUP, GROUP), 0)
    col = lax.broadcasted_iota(jnp.int32, (GROUP, GROUP), 1)
    row_chunk, col_chunk = row // CHUNK, col // CHUNK
    same_chunk = row_chunk == col_chunk
    tri_f = jnp.logical_and(same_chunk, row >= col)
    tri_b = jnp.logical_and(same_chunk, col >= row)
    cross_f = jnp.logical_and(row_chunk == col_chunk + 1, col_chunk % 2 == 0)
    cross_b = jnp.logical_and(row_chunk == col_chunk - 1, col_chunk % 2 == 1)

    def scan_step(i, carry):
        tf = pl.multiple_of(i * GROUP, GROUP)
        tb = pl.multiple_of((n_groups - 1 - i) * GROUP, GROUP)
        directions = [
            _gla_group(q_ref, k_ref, v_ref, a_ref, waf_ref[...], baf_ref[...], tri_f, cross_f, sf_ref, of_ref, tf,
                       backward=False),
            _gla_group(q_ref, k_ref, v_ref, a_ref, wab_ref[...], bab_ref[...], tri_b, cross_b, sb_ref, ob_ref, tb,
                       backward=True),
        ]
        while directions:
            directions = [d for d in directions if next(d, True) is None]
        return carry

    if n_groups == 1:
        scan_step(0, 0)
    else:
        lax.fori_loop(0, n_groups, scan_step, 0)

    def out_step(i, carry):
        t = pl.ds(pl.multiple_of(i * CHUNK, CHUNK), CHUNK)
        o = of_ref[t, :] + ob_ref[t, :]
        ms = jnp.mean(o * o, axis=-1, keepdims=True)
        o = o * lax.rsqrt(ms + EPS) * gn_ref[...]
        og_ref[t, :] = (o * _silu(g_ref[t, :].astype(F32))).astype(og_ref.dtype)
        return carry

    lax.fori_loop(0, seq_len // CHUNK, out_step, 0)

    if emit_state:
        sf_out_ref[...] = sf_ref[...]
        sb_out_ref[...] = sb_ref[...]


def _gla(proj, a_lr, w_a2f, b_af, w_a2b, b_ab, gn, batch, seq_len, s0=None, emit_state=False):
    proj3 = proj.reshape(batch, seq_len, HALF_WIDTH)
    a3 = a_lr.reshape(batch, seq_len, LANES)
    kq, kv = COL_K // GLA_HEAD_K, COL_V // GLA_HEAD_V
    kg = COL_G // GLA_HEAD_V
    in_specs = [
        pl.BlockSpec((None, seq_len, GLA_HEAD_K), lambda b, h: (b, 0, h)),
        pl.BlockSpec((None, seq_len, GLA_HEAD_K), lambda b, h: (b, 0, kq + h)),
        pl.BlockSpec((None, seq_len, GLA_HEAD_V), lambda b, h: (b, 0, kv + h)),
        pl.BlockSpec((None, seq_len, GLA_HEAD_V), lambda b, h: (b, 0, kg + h)),
        pl.BlockSpec((None, seq_len, LANES), lambda b, h: (b, 0, 0)),
        pl.BlockSpec((LANES, GLA_HEAD_K), lambda b, h: (0, h)),
        pl.BlockSpec((1, GLA_HEAD_K), lambda b, h: (0, h)),
        pl.BlockSpec((LANES, GLA_HEAD_K), lambda b, h: (0, h)),
        pl.BlockSpec((1, GLA_HEAD_K), lambda b, h: (0, h)),
        pl.BlockSpec((1, GLA_HEAD_V), lambda b, h: (0, 0)),
    ]
    args = [proj3, proj3, proj3, proj3, a3, w_a2f, b_af, w_a2b, b_ab, gn]
    state_spec = pl.BlockSpec((None, None, GLA_HEAD_K, GLA_HEAD_V), lambda b, h: (b, h, 0, 0))
    if s0 is not None:
        in_specs += [state_spec, state_spec]
        args += list(s0)
    out_specs = [pl.BlockSpec((None, seq_len, GLA_HEAD_V), lambda b, h: (b, 0, h))]
    out_shape = [jax.ShapeDtypeStruct((batch, seq_len, GLA_DV), BF16)]
    if emit_state:
        out_specs += [state_spec, state_spec]
        out_shape += [jax.ShapeDtypeStruct((batch, GLA_HEADS, GLA_HEAD_K, GLA_HEAD_V), F32)] * 2
    state_bytes = _nbytes((GLA_HEAD_K, GLA_HEAD_V), F32)
    blocks = (2 * _nbytes((seq_len, GLA_HEAD_K), BF16) + 3 * _nbytes((seq_len, GLA_HEAD_V), BF16)
              + _nbytes((seq_len, LANES), F32) + 4 * state_bytes)
    scratch = 2 * _nbytes((seq_len, GLA_HEAD_V), F32) + 2 * state_bytes
    outs = pl.pallas_call(
        functools.partial(_gla_kernel, seq_len=seq_len, has_init=s0 is not None, emit_state=emit_state),
        grid=(batch, GLA_HEADS),
        in_specs=in_specs,
        out_specs=out_specs,
        out_shape=out_shape,
        scratch_shapes=[
            pltpu.VMEM((seq_len, GLA_HEAD_V), F32),
            pltpu.VMEM((seq_len, GLA_HEAD_V), F32),
            pltpu.VMEM((GLA_HEAD_K, GLA_HEAD_V), F32),
            pltpu.VMEM((GLA_HEAD_K, GLA_HEAD_V), F32),
        ],
        compiler_params=pltpu.CompilerParams(
            dimension_semantics=("parallel", "arbitrary"),
            vmem_limit_bytes=_vmem_limit(blocks, scratch_bytes=scratch, temp_bytes=8 << 20)),
        name="gla",
    )(*args)
    og = outs[0].reshape(batch * seq_len, GLA_DV)
    return (og, outs[1], outs[2]) if emit_state else (og, None, None)


def _conv_kernel(ua_ref, ug_ref, cw_ref, cb_ref, lng_ref, lnb_ref, o_ref, pad_ref, shift_ref, conv_ref, *, seg_len):
    n_seg = CONV_TILE // seg_len
    stride = seg_len + 2 * CONV_HALO
    pad_rows = n_seg * stride
    rows = min(seg_len, 64)
    glu = ua_ref[...].astype(F32) * _sigmoid(ug_ref[...].astype(F32))
    halo = jnp.zeros((CONV_HALO, CONV_W), F32)
    for s in range(n_seg):
        base = s * stride
        pad_ref[base:base + CONV_HALO, :] = halo
        pad_ref[base + CONV_HALO:base + CONV_HALO + seg_len, :] = glu[s * seg_len:(s + 1) * seg_len, :]
        pad_ref[base + CONV_HALO + seg_len:base + stride, :] = halo
    pad_ref[pad_rows:pad_rows + SUBLANES, :] = jnp.zeros((SUBLANES, CONV_W), F32)
    for r in range(1, SUBLANES):
        shift_ref[r - 1, :, :] = pad_ref[r:pad_rows + r, :]

    def lane_block(cb, carry):
        lanes = pl.ds(pl.multiple_of(cb * LANES, LANES), LANES)
        bias = cb_ref[:, lanes]
        for s in range(n_seg):
            for rb in range(seg_len // rows):
                first = s * stride + CONV_HALO - CONV_K // 2 + rb * rows
                acc = jnp.zeros((rows, LANES), F32)
                for tap in range(CONV_K):
                    aligned, r = divmod(first + tap, SUBLANES)
                    at = pl.ds(aligned * SUBLANES, rows)
                    window = pad_ref[at, lanes] if r == 0 else shift_ref[r - 1, at, lanes]
                    acc = acc + window * cw_ref[tap:tap + 1, lanes]
                conv_ref[pl.ds(s * seg_len + rb * rows, rows), lanes] = acc + bias
        return carry

    lax.fori_loop(0, CONV_W // LANES, lane_block, 0)

    u = conv_ref[...]
    mu = jnp.mean(u, axis=-1, keepdims=True)
    uc = u - mu
    var = jnp.mean(uc * uc, axis=-1, keepdims=True)
    y = uc * lax.rsqrt(var + EPS) * lng_ref[...] + lnb_ref[...]
    o_ref[...] = _silu(y).astype(o_ref.dtype)


def _conv_module(proj, conv_w, conv_b, ln_g, ln_b, seg_len):
    m = proj.shape[0]
    n_seg = CONV_TILE // seg_len
    pad_rows = n_seg * (seg_len + 2 * CONV_HALO)
    row = lambda v: v.reshape(1, CONV_W)
    blocks = 3 * _nbytes((CONV_TILE, CONV_W), BF16)
    scratch = SUBLANES * _nbytes((pad_rows + SUBLANES, CONV_W), F32) + _nbytes((CONV_TILE, CONV_W), F32)
    return pl.pallas_call(
        functools.partial(_conv_kernel, seg_len=seg_len),
        grid=(m // CONV_TILE,),
        in_specs=[
            pl.BlockSpec((CONV_TILE, CONV_W), lambda i: (i, COL_UA // CONV_W)),
            pl.BlockSpec((CONV_TILE, CONV_W), lambda i: (i, COL_UG // CONV_W)),
            pl.BlockSpec((CONV_K, CONV_W), lambda i: (0, 0)),
            pl.BlockSpec((1, CONV_W), lambda i: (0, 0)),
            pl.BlockSpec((1, CONV_W), lambda i: (0, 0)),
            pl.BlockSpec((1, CONV_W), lambda i: (0, 0)),
        ],
        out_specs=pl.BlockSpec((CONV_TILE, CONV_W), lambda i: (i, 0)),
        out_shape=jax.ShapeDtypeStruct((m, CONV_W), BF16),
        scratch_shapes=[pltpu.VMEM((pad_rows + SUBLANES, CONV_W), F32),
                        pltpu.VMEM((SUBLANES - 1, pad_rows, CONV_W), F32),
                        pltpu.VMEM((CONV_TILE, CONV_W), F32)],
        compiler_params=pltpu.CompilerParams(
            dimension_semantics=("parallel",),
            vmem_limit_bytes=_vmem_limit(blocks, scratch_bytes=scratch,
                                         temp_bytes=4 * _nbytes((CONV_TILE, CONV_W), F32))),
        name="conv_module",
    )(proj, proj, conv_w, row(conv_b), row(ln_g), row(ln_b))


def _trunk(x, mods, row_of_tile, batch, seq_len, conv_seg, wts, s0, emit_state):
    h = _norm_mod(x, wts["norm_mix_g"], mods, 0, 1, row_of_tile)
    proj_att = _mm(h, wts["w_att"], BF16, 1024, 1024, "in_proj_att")
    proj_mix = _mm(h, wts["w_mix"], BF16, 1024, 1024, "in_proj_mix")
    a_lr = _mm(h, wts["w_decay"], F32, 1024, LANES, "decay_proj")
    og, s_f, s_b = _gla(proj_att, a_lr, wts["w_a2f"], wts["b_af"], wts["w_a2b"], wts["b_ab"], wts["gla_norm_g"],
                        batch, seq_len, s0=s0, emit_state=emit_state)
    uc = _conv_module(proj_mix, wts["conv_w"], wts["conv_b"], wts["conv_ln_g"], wts["conv_ln_b"], conv_seg)
    mix = _merge(og, uc, proj_mix, wts["w_gla_o"], wts["w_conv_o"])
    x1 = _mm_resid(mix, wts["w_out"], x, mods, 2, row_of_tile, 1024, 1024, D_MODEL, "out_proj")
    h2 = _norm_mod(x1, wts["norm_ffn_g"], mods, 3, 4, row_of_tile)
    hid = _ffn_glu(h2, wts["w_ffn1"], wts["w_ffn3"])
    x2 = _mm_resid(hid, wts["w_ffn2"], x1, mods, 5, row_of_tile, 1024, 1024, FFN_PAD // 4, "ffn_down")
    return _final_norm(x2, wts["final_norm_g"]), s_f, s_b


def kernel(x_prompt, x_sample, state_gla_fwd, state_gla_bwd, c, c_ctx, norm_mix_g, w_ada, b_ada, w_in, w_a2_fwd, b_a_fwd, w_a2_bwd, b_a_bwd, gla_norm_g, w_gla_o, conv_w, conv_b, conv_ln_g, conv_ln_b, w_conv_o, w_out, norm_ffn_g, w_ffn1, w_ffn3, w_ffn2, final_norm_g):
    n_req, seq, _ = x_prompt.shape
    dec_batch, dec_seq, _ = x_sample.shape
    depth = w_in.shape[0]
    assert depth == 1 and dec_batch + 1 <= MOD_ROWS

    w_att, w_mix, w_decay = _split_cast(w_in[0])
    pad_ffn = FFN_PAD - FFN_HIDDEN
    decay_rows = lambda w, first: jnp.pad(w.astype(BF16), ((first, LANES - GATE_RANK - first), (0, 0)))
    wts = {
        "w_att": w_att, "w_mix": w_mix, "w_decay": w_decay,
        "w_a2f": decay_rows(w_a2_fwd[0], 0),
        "w_a2b": decay_rows(w_a2_bwd[0], GATE_RANK),
        "b_af": b_a_fwd[0].reshape(1, GLA_DK),
        "b_ab": b_a_bwd[0].reshape(1, GLA_DK),
        "gla_norm_g": gla_norm_g[0].reshape(1, GLA_HEAD_V),
        "w_gla_o": w_gla_o[0].astype(BF16),
        "conv_w": conv_w[0], "conv_b": conv_b[0], "conv_ln_g": conv_ln_g[0], "conv_ln_b": conv_ln_b[0],
        "w_conv_o": w_conv_o[0].astype(BF16),
        "w_out": w_out[0].astype(BF16),
        "w_ffn1": _cast_pad(w_ffn1[0], (D_MODEL, FFN_PAD), (D_MODEL, pad_ffn)),
        "w_ffn3": _cast_pad(w_ffn3[0], (D_MODEL, FFN_PAD), (D_MODEL, pad_ffn)),
        "w_ffn2": _cast_pad(w_ffn2[0], (FFN_PAD, D_MODEL), (pad_ffn, D_MODEL)),
        "norm_mix_g": norm_mix_g[0], "norm_ffn_g": norm_ffn_g[0], "final_norm_g": final_norm_g,
    }

    cond = jnp.concatenate([c_ctx[None, :], c, jnp.zeros((MOD_ROWS - 1 - dec_batch, D_MODEL), F32)], axis=0)
    mods = _ada(cond, w_ada[0], b_ada[0]).reshape(MOD_ROWS, 1, N_MOD * D_MODEL)

    ctx_row = lambda i, tm: 0
    lat_row = lambda i, tm: 1 + (i * tm) // dec_seq

    y_p, s_f, s_b = _trunk(x_prompt.reshape(n_req * seq, D_MODEL), mods, ctx_row, n_req, seq, seq, wts,
                           None, True)
    y_s, _, _ = _trunk(x_sample.reshape(dec_batch * dec_seq, D_MODEL), mods, lat_row, dec_batch, dec_seq, GRID_W,
                       wts, (state_gla_fwd[:, 0], state_gla_bwd[:, 0]), False)
    state_shape = (n_req, depth, GLA_HEADS, GLA_HEAD_K, GLA_HEAD_V)
    return (y_p.reshape(n_req, seq, D_MODEL), y_s.reshape(dec_batch, dec_seq, D_MODEL),
            s_f.reshape(state_shape), s_b.reshape(state_shape))
```

```python
import functools

import jax
import jax.numpy as jnp
from jax import lax
from jax.experimental import pallas as pl
from jax.experimental.pallas import tpu as pltpu

F32 = jnp.float32
BF16 = jnp.bfloat16

D_MODEL = 4096
GLA_HEADS = 8
GLA_HEAD_K = 256
GLA_HEAD_V = 512
GLA_DK = GLA_HEADS * GLA_HEAD_K
GLA_DV = GLA_HEADS * GLA_HEAD_V
GATE_RANK = 16
GATE_TEMP = 16.0
CHUNK = 64
GRID_W = 64
CONV_W = D_MODEL // 2
CONV_K = 31
FFN_HIDDEN = 11008
N_MOD = 6
EPS = 1e-6

LANES = 128
SUBLANES = 8
VMEM_BYTES_V7X = 64 * 1024 * 1024

COL_Q = 0
COL_K = GLA_DK
COL_V = 2 * GLA_DK
COL_G = COL_V + GLA_DV
HALF_WIDTH = COL_G + GLA_DV
COL_UA = 0
COL_UG = CONV_W
COL_MG_GLA = 2 * CONV_W
COL_MG_CONV = COL_MG_GLA + D_MODEL
assert COL_MG_CONV + D_MODEL == HALF_WIDTH
DECAY_COL0 = HALF_WIDTH
GROUP = 256
CHUNKS_PER_GROUP = GROUP // CHUNK
FFN_PAD = 11264
MOD_ROWS = 8
CONV_TILE = 256
CONV_HALO = 16


def _vmem_limit(block_bytes, scratch_bytes=0, temp_bytes=0):
    return int(min(VMEM_BYTES_V7X - (2 << 20), 2 * block_bytes + scratch_bytes + temp_bytes + (4 << 20)))


def _nbytes(shape, dtype):
    n = 1
    for s in shape:
        n *= s
    return n * jnp.dtype(dtype).itemsize


def _sigmoid(x):
    return jax.nn.sigmoid(x)


def _silu(x):
    return x * jax.nn.sigmoid(x)


def _cast_pad_kernel(x_ref, o_ref, *, valid_blocks):
    inside = jnp.logical_and(pl.program_id(0) < valid_blocks[0], pl.program_id(1) < valid_blocks[1])

    @pl.when(inside)
    def _():
        o_ref[...] = x_ref[...].astype(o_ref.dtype)

    @pl.when(jnp.logical_not(inside))
    def _():
        o_ref[...] = jnp.zeros_like(o_ref)


def _cast_pad(w, out_shape, block):
    assert all(s % b == 0 and o % b == 0 for s, o, b in zip(w.shape, out_shape, block))
    valid = (w.shape[0] // block[0], w.shape[1] // block[1])
    return pl.pallas_call(
        functools.partial(_cast_pad_kernel, valid_blocks=valid),
        grid=(out_shape[0] // block[0], out_shape[1] // block[1]),
        in_specs=[pl.BlockSpec(block, lambda i, j: (jnp.minimum(i, valid[0] - 1), jnp.minimum(j, valid[1] - 1)))],
        out_specs=pl.BlockSpec(block, lambda i, j: (i, j)),
        out_shape=jax.ShapeDtypeStruct(out_shape, BF16),
        compiler_params=pltpu.CompilerParams(
            dimension_semantics=("parallel", "parallel"),
            vmem_limit_bytes=_vmem_limit(_nbytes(block, F32) + _nbytes(block, BF16))),
        name="cast_pad",
    )(w)


def _cast_transposed_kernel(wt_ref, o_ref, *, keep):
    w = wt_ref[...].T
    if keep < w.shape[1]:
        w = jnp.where(lax.broadcasted_iota(jnp.int32, w.shape, 1) < keep, w, 0.0)
    o_ref[...] = w.astype(o_ref.dtype)


def _cast_transposed(w_t, first_row, n_rows, tr=512, keep=None):
    k = w_t.shape[1]
    assert first_row % SUBLANES == 0 and n_rows % tr == 0 and (keep is None or n_rows == tr)
    return pl.pallas_call(
        functools.partial(_cast_transposed_kernel, keep=tr if keep is None else keep),
        grid=(n_rows // tr,),
        in_specs=[pl.BlockSpec((pl.Element(tr), pl.Element(k)), lambda j: (pl.multiple_of(first_row + j * tr, SUBLANES), 0))],
        out_specs=pl.BlockSpec((k, tr), lambda j: (0, j)),
        out_shape=jax.ShapeDtypeStruct((k, n_rows), BF16),
        compiler_params=pltpu.CompilerParams(
            dimension_semantics=("parallel",),
            vmem_limit_bytes=_vmem_limit(_nbytes((tr, k), F32) + _nbytes((k, tr), BF16),
                                         temp_bytes=2 * _nbytes((tr, k), F32))),
        name="cast_transposed",
    )(w_t)


def _ada_kernel(c_ref, w_ref, b_ref, o_ref):
    s = _silu(c_ref[...]).astype(BF16)
    o_ref[...] = jnp.dot(s, w_ref[...].astype(BF16), preferred_element_type=F32) + b_ref[...]


def _ada(cond, w_ada, b_ada):
    n = w_ada.shape[1]
    tn = 512
    return pl.pallas_call(
        _ada_kernel,
        grid=(n // tn,),
        in_specs=[
            pl.BlockSpec((MOD_ROWS, D_MODEL), lambda j: (0, 0)),
            pl.BlockSpec((D_MODEL, tn), lambda j: (0, j)),
            pl.BlockSpec((1, tn), lambda j: (0, j)),
        ],
        out_specs=pl.BlockSpec((MOD_ROWS, tn), lambda j: (0, j)),
        out_shape=jax.ShapeDtypeStruct((MOD_ROWS, n), F32),
        compiler_params=pltpu.CompilerParams(
            dimension_semantics=("arbitrary",),
            vmem_limit_bytes=_vmem_limit(_nbytes((D_MODEL, tn), F32), temp_bytes=_nbytes((D_MODEL, tn), BF16))),
        name="ada",
    )(cond, w_ada, b_ada.reshape(1, n))


def _norm_mod_kernel(x_ref, g_ref, sc_ref, sh_ref, o_ref):
    x = x_ref[...]
    ms = jnp.mean(x * x, axis=-1, keepdims=True)
    h = x * lax.rsqrt(ms + EPS) * g_ref[...]
    o_ref[...] = (h * (1.0 + sc_ref[...]) + sh_ref[...]).astype(o_ref.dtype)


def _rmsnorm_kernel(x_ref, g_ref, o_ref):
    x = x_ref[...]
    ms = jnp.mean(x * x, axis=-1, keepdims=True)
    o_ref[...] = x * lax.rsqrt(ms + EPS) * g_ref[...]


def _mod_spec(which, tn, row_of_tile):
    nb = D_MODEL // tn
    return pl.BlockSpec((None, 1, tn), lambda i, j, *_: (row_of_tile(i), 0, which * nb + j))


def _norm_mod(x, g, mods, which_shift, which_scale, row_of_tile, tm=256):
    m = x.shape[0]
    row = lambda i: row_of_tile(i, tm)
    return pl.pallas_call(
        _norm_mod_kernel,
        grid=(m // tm, 1),
        in_specs=[
            pl.BlockSpec((tm, D_MODEL), lambda i, j: (i, 0)),
            pl.BlockSpec((1, D_MODEL), lambda i, j: (0, 0)),
            _mod_spec(which_scale, D_MODEL, row),
            _mod_spec(which_shift, D_MODEL, row),
        ],
        out_specs=pl.BlockSpec((tm, D_MODEL), lambda i, j: (i, 0)),
        out_shape=jax.ShapeDtypeStruct((m, D_MODEL), BF16),
        compiler_params=pltpu.CompilerParams(
            dimension_semantics=("parallel", "arbitrary"),
            vmem_limit_bytes=_vmem_limit(_nbytes((tm, D_MODEL), F32) + _nbytes((tm, D_MODEL), BF16),
                                         temp_bytes=2 * _nbytes((tm, D_MODEL), F32))),
        name="norm_mod",
    )(x, g.reshape(1, D_MODEL), mods, mods)


def _final_norm(x, g, tm=256):
    m = x.shape[0]
    return pl.pallas_call(
        _rmsnorm_kernel,
        grid=(m // tm,),
        in_specs=[pl.BlockSpec((tm, D_MODEL), lambda i: (i, 0)), pl.BlockSpec((1, D_MODEL), lambda i: (0, 0))],
        out_specs=pl.BlockSpec((tm, D_MODEL), lambda i: (i, 0)),
        out_shape=jax.ShapeDtypeStruct((m, D_MODEL), F32),
        compiler_params=pltpu.CompilerParams(
            dimension_semantics=("parallel",),
            vmem_limit_bytes=_vmem_limit(2 * _nbytes((tm, D_MODEL), F32), temp_bytes=_nbytes((tm, D_MODEL), F32))),
        name="final_norm",
    )(x, g.reshape(1, D_MODEL))


def _mm_kernel(x_ref, w_ref, o_ref):
    o_ref[...] = jnp.dot(x_ref[...], w_ref[...], preferred_element_type=F32).astype(o_ref.dtype)


def _mm(x, w, out_dtype, tm, tn, name):
    m, k = x.shape
    n = w.shape[1]
    blocks = _nbytes((tm, k), BF16) + _nbytes((k, tn), BF16) + _nbytes((tm, tn), out_dtype)
    return pl.pallas_call(
        _mm_kernel,
        grid=(m // tm, n // tn),
        in_specs=[pl.BlockSpec((tm, k), lambda i, j: (i, 0)), pl.BlockSpec((k, tn), lambda i, j: (0, j))],
        out_specs=pl.BlockSpec((tm, tn), lambda i, j: (i, j)),
        out_shape=jax.ShapeDtypeStruct((m, n), out_dtype),
        compiler_params=pltpu.CompilerParams(
            dimension_semantics=("parallel", "parallel"),
            vmem_limit_bytes=_vmem_limit(blocks, temp_bytes=_nbytes((tm, tn), F32))),
        name=name,
    )(x, w)


def _merge_kernel(og_ref, uc_ref, wg_ref, wc_ref, mgg_ref, mgc_ref, o_ref):
    y_gla = jnp.dot(og_ref[...], wg_ref[...], preferred_element_type=F32)
    y_conv = jnp.dot(uc_ref[...], wc_ref[...], preferred_element_type=F32)
    mix = _sigmoid(mgg_ref[...].astype(F32)) * y_gla + _sigmoid(mgc_ref[...].astype(F32)) * y_conv
    o_ref[...] = mix.astype(o_ref.dtype)


def _merge(og, uc, proj, w_gla_o, w_conv_o, tm=1024, tn=512):
    m = og.shape[0]
    blocks = (_nbytes((tm, GLA_DV), BF16) + _nbytes((tm, CONV_W), BF16) + _nbytes((GLA_DV, tn), BF16)
              + _nbytes((CONV_W, tn), BF16) + 3 * _nbytes((tm, tn), BF16))
    return pl.pallas_call(
        _merge_kernel,
        grid=(m // tm, D_MODEL // tn),
        in_specs=[
            pl.BlockSpec((tm, GLA_DV), lambda i, j: (i, 0)),
            pl.BlockSpec((tm, CONV_W), lambda i, j: (i, 0)),
            pl.BlockSpec((GLA_DV, tn), lambda i, j: (0, j)),
            pl.BlockSpec((CONV_W, tn), lambda i, j: (0, j)),
            pl.BlockSpec((tm, tn), lambda i, j: (i, COL_MG_GLA // tn + j)),
            pl.BlockSpec((tm, tn), lambda i, j: (i, COL_MG_CONV // tn + j)),
        ],
        out_specs=pl.BlockSpec((tm, tn), lambda i, j: (i, j)),
        out_shape=jax.ShapeDtypeStruct((m, D_MODEL), BF16),
        compiler_params=pltpu.CompilerParams(
            dimension_semantics=("parallel", "parallel"),
            vmem_limit_bytes=_vmem_limit(blocks, temp_bytes=3 * _nbytes((tm, tn), F32))),
        name="merge",
    )(og, uc, w_gla_o, w_conv_o, proj, proj)


def _mm_resid_kernel(x_ref, w_ref, r_ref, gt_ref, o_ref, *acc, nk):
    part = jnp.dot(x_ref[...], w_ref[...], preferred_element_type=F32)
    if nk == 1:
        o_ref[...] = r_ref[...] + gt_ref[...] * part
        return
    acc_ref, = acc
    kk = pl.program_id(2)

    @pl.when(kk == 0)
    def _():
        acc_ref[...] = part

    @pl.when(jnp.logical_and(kk > 0, kk < nk - 1))
    def _():
        acc_ref[...] += part

    @pl.when(kk == nk - 1)
    def _():
        o_ref[...] = r_ref[...] + gt_ref[...] * (acc_ref[...] + part)


def _mm_resid(x, w, resid, mods, which_gate, row_of_tile, tm, tn, tk, name):
    m, k = x.shape
    n = w.shape[1]
    nk = k // tk
    row = lambda i: row_of_tile(i, tm)
    blocks = _nbytes((tm, tk), BF16) + _nbytes((tk, tn), BF16) + 2 * _nbytes((tm, tn), F32)
    return pl.pallas_call(
        functools.partial(_mm_resid_kernel, nk=nk),
        grid=(m // tm, n // tn, nk),
        in_specs=[
            pl.BlockSpec((tm, tk), lambda i, j, kk: (i, kk)),
            pl.BlockSpec((tk, tn), lambda i, j, kk: (kk, j)),
            pl.BlockSpec((tm, tn), lambda i, j, kk: (i, j)),
            _mod_spec(which_gate, tn, row),
        ],
        out_specs=pl.BlockSpec((tm, tn), lambda i, j, kk: (i, j)),
        out_shape=jax.ShapeDtypeStruct((m, n), F32),
        scratch_shapes=[] if nk == 1 else [pltpu.VMEM((tm, tn), F32)],
        compiler_params=pltpu.CompilerParams(
            dimension_semantics=("parallel", "parallel", "arbitrary"),
            vmem_limit_bytes=_vmem_limit(blocks, scratch_bytes=_nbytes((tm, tn), F32),
                                         temp_bytes=_nbytes((tm, tn), F32))),
        name=name,
    )(x, w, resid, mods)


def _ffn_glu_kernel(x_ref, w1_ref, w3_ref, o_ref):
    x = x_ref[...]
    a = jnp.dot(x, w1_ref[...], preferred_element_type=F32)
    b = jnp.dot(x, w3_ref[...], preferred_element_type=F32)
    o_ref[...] = (_silu(a) * b).astype(o_ref.dtype)


def _ffn_glu(x, w1, w3, tm=1024, tn=512):
    m, k = x.shape
    n = w1.shape[1]
    blocks = _nbytes((tm, k), BF16) + 2 * _nbytes((k, tn), BF16) + _nbytes((tm, tn), BF16)
    return pl.pallas_call(
        _ffn_glu_kernel,
        grid=(m // tm, n // tn),
        in_specs=[
            pl.BlockSpec((tm, k), lambda i, j: (i, 0)),
            pl.BlockSpec((k, tn), lambda i, j: (0, j)),
            pl.BlockSpec((k, tn), lambda i, j: (0, j)),
        ],
        out_specs=pl.BlockSpec((tm, tn), lambda i, j: (i, j)),
        out_shape=jax.ShapeDtypeStruct((m, n), BF16),
        compiler_params=pltpu.CompilerParams(
            dimension_semantics=("parallel", "parallel"),
            vmem_limit_bytes=_vmem_limit(blocks, temp_bytes=3 * _nbytes((tm, tn), F32))),
        name="ffn_glu",
    )(x, w1, w3)


def _split_bf16(x):
    hi = x.astype(BF16)
    lo = (x - hi.astype(F32)).astype(BF16)
    return hi, lo


def _gla_group(q_ref, k_ref, v_ref, a_ref, w_a2, b_a, tri, cross, s_ref, o_ref, t0, backward):
    rows = pl.ds(t0, GROUP)
    tn_dims = (((0,), (0,)), ((), ()))
    z = jnp.dot(a_ref[rows, :].astype(BF16), w_a2, preferred_element_type=F32) + b_a
    log_a = (jnp.minimum(z, 0.0) - jnp.log(1.0 + jnp.exp(-jnp.abs(z)))) * (1.0 / GATE_TEMP)
    hi, lo = _split_bf16(log_a)
    yield
    tri_bf = tri.astype(BF16)
    b = jnp.dot(tri_bf, hi, preferred_element_type=F32) + jnp.dot(tri_bf, lo, preferred_element_type=F32)
    yield
    nt_dims = (((1,), (1,)), ((), ()))
    totals = [b[c * CHUNK:c * CHUNK + 1, :] if backward else b[(c + 1) * CHUNK - 1:(c + 1) * CHUNK, :]
              for c in range(CHUNKS_PER_GROUP)]
    decays = [jnp.exp(t) for t in totals]
    scanned_first = lambda c: (c % 2 == 1) if backward else (c % 2 == 0)
    over_chunk = lambda r: jnp.broadcast_to(r, (CHUNK, GLA_HEAD_K))
    one = jnp.ones((CHUNK, GLA_HEAD_K), F32)
    chunks = range(CHUNKS_PER_GROUP)
    b_end = jnp.concatenate([over_chunk(totals[c]) for c in chunks], axis=0)
    q_scale = jnp.concatenate([one if scanned_first(c) else over_chunk(decays[c ^ 1]) for c in chunks], axis=0)
    k_scale = jnp.concatenate([over_chunk(decays[c ^ 1]) if scanned_first(c) else one for c in chunks], axis=0)
    tot_cols = jnp.concatenate(totals + [jnp.zeros((LANES - CHUNKS_PER_GROUP, GLA_HEAD_K), F32)], axis=0).T

    qf = q_ref[rows, :].astype(F32) * (GLA_HEAD_K ** -0.5)
    kf = k_ref[rows, :].astype(F32)
    vb = v_ref[rows, :]
    q_f32 = qf * jnp.exp(b)
    k_f32 = kf * jnp.exp(b_end - b)
    q_dec = q_f32.astype(BF16)
    q_pair = (q_f32 * q_scale).astype(BF16)
    k_dec = (kf * jnp.exp(-b)).astype(BF16)
    k_end = k_f32.astype(BF16)
    k_pair = (k_f32 * k_scale).astype(BF16)
    yield
    within = lax.dot_general(q_dec, k_dec, nt_dims, preferred_element_type=F32)
    across = lax.dot_general(q_dec, k_end, nt_dims, preferred_element_type=F32)
    scores = jnp.where(tri, within, jnp.where(cross, across, 0.0)).astype(BF16)
    o_intra = jnp.dot(scores, vb, preferred_element_type=F32)
    yield

    n_pairs = CHUNKS_PER_GROUP // 2
    for p in (range(n_pairs - 1, -1, -1) if backward else range(n_pairs)):
        pr = slice(2 * p * CHUNK, 2 * (p + 1) * CHUNK)
        decay = jnp.exp(tot_cols[:, 2 * p:2 * p + 1] + tot_cols[:, 2 * p + 1:2 * p + 2])
        s = s_ref[...]
        o_ref[pl.ds(t0 + 2 * p * CHUNK, 2 * CHUNK), :] = o_intra[pr, :] + jnp.dot(
            q_pair[pr, :], s.astype(BF16), preferred_element_type=F32)
        s_ref[...] = decay * s + lax.dot_general(k_pair[pr, :], vb[pr, :], tn_dims, preferred_element_type=F32)
        yield


def _gla_kernel(*refs, seq_len, has_init, emit_state):
    (q_ref, k_ref, v_ref, g_ref, a_ref, waf_ref, baf_ref, wab_ref, bab_ref, gn_ref), refs = refs[:10], refs[10:]
    if has_init:
        (s0f_ref, s0b_ref), refs = refs[:2], refs[2:]
    og_ref, refs = refs[0], refs[1:]
    if emit_state:
        (sf_out_ref, sb_out_ref), refs = refs[:2], refs[2:]
    of_ref, ob_ref, sf_ref, sb_ref = refs
    n_groups = seq_len // GROUP

    if has_init:
        sf_ref[...] = s0f_ref[...]
        sb_ref[...] = s0b_ref[...]
    else:
        sf_ref[...] = jnp.zeros_like(sf_ref)
        sb_ref[...] = jnp.zeros_like(sb_ref)

    row = lax.broadcasted_iota(jnp.int32, (GROUP, GROUP), 0)
    col = lax.broadcasted_iota(jnp.int32, (GROUP, GROUP), 1)
    row_chunk, col_chunk = row // CHUNK, col // CHUNK
    same_chunk = row_chunk == col_chunk
    tri_f = jnp.logical_and(same_chunk, row >= col)
    tri_b = jnp.logical_and(same_chunk, col >= row)
    cross_f = jnp.logical_and(row_chunk == col_chunk + 1, col_chunk % 2 == 0)
    cross_b = jnp.logical_and(row_chunk == col_chunk - 1, col_chunk % 2 == 1)

    def scan_step(i, carry):
        tf = pl.multiple_of(i * GROUP, GROUP)
        tb = pl.multiple_of((n_groups - 1 - i) * GROUP, GROUP)
        directions = [
            _gla_group(q_ref, k_ref, v_ref, a_ref, waf_ref[...], baf_ref[...], tri_f, cross_f, sf_ref, of_ref, tf,
                       backward=False),
            _gla_group(q_ref, k_ref, v_ref, a_ref, wab_ref[...], bab_ref[...], tri_b, cross_b, sb_ref, ob_ref, tb,
                       backward=True),
        ]
        while directions:
            directions = [d for d in directions if next(d, True) is None]
        return carry

    if n_groups == 1:
        scan_step(0, 0)
    else:
        lax.fori_loop(0, n_groups, scan_step, 0)

    def out_step(i, carry):
        t = pl.ds(pl.multiple_of(i * CHUNK, CHUNK), CHUNK)
        o = of_ref[t, :] + ob_ref[t, :]
        ms = jnp.mean(o * o, axis=-1, keepdims=True)
        o = o * lax.rsqrt(ms + EPS) * gn_ref[...]
        og_ref[t, :] = (o * _silu(g_ref[t, :].astype(F32))).astype(og_ref.dtype)
        return carry

    lax.fori_loop(0, seq_len // CHUNK, out_step, 0)

    if emit_state:
        sf_out_ref[...] = sf_ref[...]
        sb_out_ref[...] = sb_ref[...]


def _gla(proj, a_lr, w_a2f, b_af, w_a2b, b_ab, gn, batch, seq_len, s0=None, emit_state=False):
    proj3 = proj.reshape(batch, seq_len, HALF_WIDTH)
    a3 = a_lr.reshape(batch, seq_len, LANES)
    kq, kv = COL_K // GLA_HEAD_K, COL_V // GLA_HEAD_V
    kg = COL_G // GLA_HEAD_V
    in_specs = [
        pl.BlockSpec((None, seq_len, GLA_HEAD_K), lambda b, h: (b, 0, h)),
        pl.BlockSpec((None, seq_len, GLA_HEAD_K), lambda b, h: (b, 0, kq + h)),
        pl.BlockSpec((None, seq_len, GLA_HEAD_V), lambda b, h: (b, 0, kv + h)),
        pl.BlockSpec((None, seq_len, GLA_HEAD_V), lambda b, h: (b, 0, kg + h)),
        pl.BlockSpec((None, seq_len, LANES), lambda b, h: (b, 0, 0)),
        pl.BlockSpec((LANES, GLA_HEAD_K), lambda b, h: (0, h)),
        pl.BlockSpec((1, GLA_HEAD_K), lambda b, h: (0, h)),
        pl.BlockSpec((LANES, GLA_HEAD_K), lambda b, h: (0, h)),
        pl.BlockSpec((1, GLA_HEAD_K), lambda b, h: (0, h)),
        pl.BlockSpec((1, GLA_HEAD_V), lambda b, h: (0, 0)),
    ]
    args = [proj3, proj3, proj3, proj3, a3, w_a2f, b_af, w_a2b, b_ab, gn]
    state_spec = pl.BlockSpec((None, None, GLA_HEAD_K, GLA_HEAD_V), lambda b, h: (b, h, 0, 0))
    if s0 is not None:
        in_specs += [state_spec, state_spec]
        args += list(s0)
    out_specs = [pl.BlockSpec((None, seq_len, GLA_HEAD_V), lambda b, h: (b, 0, h))]
    out_shape = [jax.ShapeDtypeStruct((batch, seq_len, GLA_DV), BF16)]
    if emit_state:
        out_specs += [state_spec, state_spec]
        out_shape += [jax.ShapeDtypeStruct((batch, GLA_HEADS, GLA_HEAD_K, GLA_HEAD_V), F32)] * 2
    state_bytes = _nbytes((GLA_HEAD_K, GLA_HEAD_V), F32)
    blocks = (2 * _nbytes((seq_len, GLA_HEAD_K), BF16) + 3 * _nbytes((seq_len, GLA_HEAD_V), BF16)
              + _nbytes((seq_len, LANES), F32) + 4 * state_bytes)
    scratch = 2 * _nbytes((seq_len, GLA_HEAD_V), F32) + 2 * state_bytes
    outs = pl.pallas_call(
        functools.partial(_gla_kernel, seq_len=seq_len, has_init=s0 is not None, emit_state=emit_state),
        grid=(batch, GLA_HEADS),
        in_specs=in_specs,
        out_specs=out_specs,
        out_shape=out_shape,
        scratch_shapes=[
            pltpu.VMEM((seq_len, GLA_HEAD_V), F32),
            pltpu.VMEM((seq_len, GLA_HEAD_V), F32),
            pltpu.VMEM((GLA_HEAD_K, GLA_HEAD_V), F32),
            pltpu.VMEM((GLA_HEAD_K, GLA_HEAD_V), F32),
        ],
        compiler_params=pltpu.CompilerParams(
            dimension_semantics=("parallel", "arbitrary"),
            vmem_limit_bytes=_vmem_limit(blocks, scratch_bytes=scratch, temp_bytes=8 << 20)),
        name="gla",
    )(*args)
    og = outs[0].reshape(batch * seq_len, GLA_DV)
    return (og, outs[1], outs[2]) if emit_state else (og, None, None)


def _conv_kernel(ua_ref, ug_ref, cw_ref, cb_ref, lng_ref, lnb_ref, o_ref, pad_ref, shift_ref, conv_ref, *, seg_len):
    n_seg = CONV_TILE // seg_len
    stride = seg_len + 2 * CONV_HALO
    pad_rows = n_seg * stride
    rows = min(seg_len, 64)
    glu = ua_ref[...].astype(F32) * _sigmoid(ug_ref[...].astype(F32))
    halo = jnp.zeros((CONV_HALO, CONV_W), F32)
    for s in range(n_seg):
        base = s * stride
        pad_ref[base:base + CONV_HALO, :] = halo
        pad_ref[base + CONV_HALO:base + CONV_HALO + seg_len, :] = glu[s * seg_len:(s + 1) * seg_len, :]
        pad_ref[base + CONV_HALO + seg_len:base + stride, :] = halo
    pad_ref[pad_rows:pad_rows + SUBLANES, :] = jnp.zeros((SUBLANES, CONV_W), F32)
    for r in range(1, SUBLANES):
        shift_ref[r - 1, :, :] = pad_ref[r:pad_rows + r, :]

    def lane_block(cb, carry):
        lanes = pl.ds(pl.multiple_of(cb * LANES, LANES), LANES)
        bias = cb_ref[:, lanes]
        for s in range(n_seg):
            for rb in range(seg_len // rows):
                first = s * stride + CONV_HALO - CONV_K // 2 + rb * rows
                acc = jnp.zeros((rows, LANES), F32)
                for tap in range(CONV_K):
                    aligned, r = divmod(first + tap, SUBLANES)
                    at = pl.ds(aligned * SUBLANES, rows)
                    window = pad_ref[at, lanes] if r == 0 else shift_ref[r - 1, at, lanes]
                    acc = acc + window * cw_ref[tap:tap + 1, lanes]
                conv_ref[pl.ds(s * seg_len + rb * rows, rows), lanes] = acc + bias
        return carry

    lax.fori_loop(0, CONV_W // LANES, lane_block, 0)

    u = conv_ref[...]
    mu = jnp.mean(u, axis=-1, keepdims=True)
    uc = u - mu
    var = jnp.mean(uc * uc, axis=-1, keepdims=True)
    y = uc * lax.rsqrt(var + EPS) * lng_ref[...] + lnb_ref[...]
    o_ref[...] = _silu(y).astype(o_ref.dtype)


def _conv_module(proj, conv_w, conv_b, ln_g, ln_b, seg_len):
    m = proj.shape[0]
    n_seg = CONV_TILE // seg_len
    pad_rows = n_seg * (seg_len + 2 * CONV_HALO)
    row = lambda v: v.reshape(1, CONV_W)
    blocks = 3 * _nbytes((CONV_TILE, CONV_W), BF16)
    scratch = SUBLANES * _nbytes((pad_rows + SUBLANES, CONV_W), F32) + _nbytes((CONV_TILE, CONV_W), F32)
    return pl.pallas_call(
        functools.partial(_conv_kernel, seg_len=seg_len),
        grid=(m // CONV_TILE,),
        in_specs=[
            pl.BlockSpec((CONV_TILE, CONV_W), lambda i: (i, COL_UA // CONV_W)),
            pl.BlockSpec((CONV_TILE, CONV_W), lambda i: (i, COL_UG // CONV_W)),
            pl.BlockSpec((CONV_K, CONV_W), lambda i: (0, 0)),
            pl.BlockSpec((1, CONV_W), lambda i: (0, 0)),
            pl.BlockSpec((1, CONV_W), lambda i: (0, 0)),
            pl.BlockSpec((1, CONV_W), lambda i: (0, 0)),
        ],
        out_specs=pl.BlockSpec((CONV_TILE, CONV_W), lambda i: (i, 0)),
        out_shape=jax.ShapeDtypeStruct((m, CONV_W), BF16),
        scratch_shapes=[pltpu.VMEM((pad_rows + SUBLANES, CONV_W), F32),
                        pltpu.VMEM((SUBLANES - 1, pad_rows, CONV_W), F32),
                        pltpu.VMEM((CONV_TILE, CONV_W), F32)],
        compiler_params=pltpu.CompilerParams(
            dimension_semantics=("parallel",),
            vmem_limit_bytes=_vmem_limit(blocks, scratch_bytes=scratch,
                                         temp_bytes=4 * _nbytes((CONV_TILE, CONV_W), F32))),
        name="conv_module",
    )(proj, proj, conv_w, row(conv_b), row(ln_g), row(ln_b))


def _trunk(x, mods, row_of_tile, batch, seq_len, conv_seg, wts, s0, emit_state):
    h = _norm_mod(x, wts["norm_mix_g"], mods, 0, 1, row_of_tile)
    proj_att = _mm(h, wts["w_att"], BF16, 1024, 1024, "in_proj_att")
    proj_mix = _mm(h, wts["w_mix"], BF16, 1024, 1024, "in_proj_mix")
    a_lr = _mm(h, wts["w_decay"], F32, 1024, LANES, "decay_proj")
    og, s_f, s_b = _gla(proj_att, a_lr, wts["w_a2f"], wts["b_af"], wts["w_a2b"], wts["b_ab"], wts["gla_norm_g"],
                        batch, seq_len, s0=s0, emit_state=emit_state)
    uc = _conv_module(proj_mix, wts["conv_w"], wts["conv_b"], wts["conv_ln_g"], wts["conv_ln_b"], conv_seg)
    mix = _merge(og, uc, proj_mix, wts["w_gla_o"], wts["w_conv_o"])
    x1 = _mm_resid(mix, wts["w_out"], x, mods, 2, row_of_tile, 1024, 1024, D_MODEL, "out_proj")
    h2 = _norm_mod(x1, wts["norm_ffn_g"], mods, 3, 4, row_of_tile)
    hid = _ffn_glu(h2, wts["w_ffn1"], wts["w_ffn3"])
    x2 = _mm_resid(hid, wts["w_ffn2"], x1, mods, 5, row_of_tile, 1024, 1024, FFN_PAD // 4, "ffn_down")
    return _final_norm(x2, wts["final_norm_g"]), s_f, s_b


def kernel(x_prompt, x_sample, state_gla_fwd, state_gla_bwd, c, c_ctx, norm_mix_g, w_ada, b_ada, w_in, w_a2_fwd, b_a_fwd, w_a2_bwd, b_a_bwd, gla_norm_g, w_gla_o, conv_w, conv_b, conv_ln_g, conv_ln_b, w_conv_o, w_out, norm_ffn_g, w_ffn1, w_ffn3, w_ffn2, final_norm_g):
    n_req, seq, _ = x_prompt.shape
    dec_batch, dec_seq, _ = x_sample.shape
    depth = w_in.shape[0]
    assert depth == 1 and dec_batch + 1 <= MOD_ROWS

    w_in_t = jnp.transpose(w_in[0])
    w_att = _cast_transposed(w_in_t, 0, HALF_WIDTH)
    w_mix = _cast_transposed(w_in_t, DECAY_COL0 + 2 * GATE_RANK, HALF_WIDTH)
    w_decay = _cast_transposed(w_in_t, DECAY_COL0, LANES, tr=LANES, keep=2 * GATE_RANK)
    pad_ffn = FFN_PAD - FFN_HIDDEN
    decay_rows = lambda w, first: jnp.pad(w.astype(BF16), ((first, LANES - GATE_RANK - first), (0, 0)))
    wts = {
        "w_att": w_att, "w_mix": w_mix, "w_decay": w_decay,
        "w_a2f": decay_rows(w_a2_fwd[0], 0),
        "w_a2b": decay_rows(w_a2_bwd[0], GATE_RANK),
        "b_af": b_a_fwd[0].reshape(1, GLA_DK),
        "b_ab": b_a_bwd[0].reshape(1, GLA_DK),
        "gla_norm_g": gla_norm_g[0].reshape(1, GLA_HEAD_V),
        "w_gla_o": w_gla_o[0].astype(BF16),
        "conv_w": conv_w[0], "conv_b": conv_b[0], "conv_ln_g": conv_ln_g[0], "conv_ln_b": conv_ln_b[0],
        "w_conv_o": w_conv_o[0].astype(BF16),
        "w_out": w_out[0].astype(BF16),
        "w_ffn1": _cast_pad(w_ffn1[0], (D_MODEL, FFN_PAD), (D_MODEL, pad_ffn)),
        "w_ffn3": _cast_pad(w_ffn3[0], (D_MODEL, FFN_PAD), (D_MODEL, pad_ffn)),
        "w_ffn2": _cast_pad(w_ffn2[0], (FFN_PAD, D_MODEL), (pad_ffn, D_MODEL)),
        "norm_mix_g": norm_mix_g[0], "norm_ffn_g": norm_ffn_g[0], "final_norm_g": final_norm_g,
    }

    cond = jnp.concatenate([c_ctx[None, :], c, jnp.zeros((MOD_ROWS - 1 - dec_batch, D_MODEL), F32)], axis=0)
    mods = _ada(cond, w_ada[0], b_ada[0]).reshape(MOD_ROWS, 1, N_MOD * D_MODEL)

    ctx_row = lambda i, tm: 0
    lat_row = lambda i, tm: 1 + (i * tm) // dec_seq

    y_p, s_f, s_b = _trunk(x_prompt.reshape(n_req * seq, D_MODEL), mods, ctx_row, n_req, seq, seq, wts,
                           None, True)
    y_s, _, _ = _trunk(x_sample.reshape(dec_batch * dec_seq, D_MODEL), mods, lat_row, dec_batch, dec_seq, GRID_W,
                       wts, (state_gla_fwd[:, 0], state_gla_bwd[:, 0]), False)
    state_shape = (n_req, depth, GLA_HEADS, GLA_HEAD_K, GLA_HEAD_V)
    return (y_p.reshape(n_req, seq, D_MODEL), y_s.reshape(dec_batch, dec_seq, D_MODEL),
            s_f.reshape(state_shape), s_b.reshape(state_shape))
```

```python
import functools

import jax
import jax.numpy as jnp
from jax import lax
from jax.experimental import pallas as pl
from jax.experimental.pallas import tpu as pltpu

F32 = jnp.float32
BF16 = jnp.bfloat16

D_MODEL = 4096
GLA_HEADS = 8
GLA_HEAD_K = 256
GLA_HEAD_V = 512
GLA_DK = GLA_HEADS * GLA_HEAD_K
GLA_DV = GLA_HEADS * GLA_HEAD_V
GATE_RANK = 16
GATE_TEMP = 16.0
CHUNK = 64
GRID_W = 64
CONV_W = D_MODEL // 2
CONV_K = 31
FFN_HIDDEN = 11008
N_MOD = 6
EPS = 1e-6

LANES = 128
SUBLANES = 8
VMEM_BYTES_V7X = 64 * 1024 * 1024

COL_Q = 0
COL_K = GLA_DK
COL_V = 2 * GLA_DK
COL_G = COL_V + GLA_DV
HALF_WIDTH = COL_G + GLA_DV
COL_UA = 0
COL_UG = CONV_W
COL_MG_GLA = 2 * CONV_W
COL_MG_CONV = COL_MG_GLA + D_MODEL
assert COL_MG_CONV + D_MODEL == HALF_WIDTH
DECAY_COL0 = HALF_WIDTH
GROUP = 256
CHUNKS_PER_GROUP = GROUP // CHUNK
GLA_UNITS = 8
FFN_PAD = 11264
MOD_ROWS = 8
CONV_TILE = 256
CONV_HALO = 16


def _vmem_limit(block_bytes, scratch_bytes=0, temp_bytes=0):
    return int(min(VMEM_BYTES_V7X - (2 << 20), 2 * block_bytes + scratch_bytes + temp_bytes + (4 << 20)))


def _nbytes(shape, dtype):
    n = 1
    for s in shape:
        n *= s
    return n * jnp.dtype(dtype).itemsize


def _sigmoid(x):
    return jax.nn.sigmoid(x)


def _silu(x):
    return x * jax.nn.sigmoid(x)


def _cast_pad_kernel(x_ref, o_ref, *, valid_blocks):
    inside = jnp.logical_and(pl.program_id(0) < valid_blocks[0], pl.program_id(1) < valid_blocks[1])

    @pl.when(inside)
    def _():
        o_ref[...] = x_ref[...].astype(o_ref.dtype)

    @pl.when(jnp.logical_not(inside))
    def _():
        o_ref[...] = jnp.zeros_like(o_ref)


def _cast_pad(w, out_shape, block):
    assert all(s % b == 0 and o % b == 0 for s, o, b in zip(w.shape, out_shape, block))
    valid = (w.shape[0] // block[0], w.shape[1] // block[1])
    return pl.pallas_call(
        functools.partial(_cast_pad_kernel, valid_blocks=valid),
        grid=(out_shape[0] // block[0], out_shape[1] // block[1]),
        in_specs=[pl.BlockSpec(block, lambda i, j: (jnp.minimum(i, valid[0] - 1), jnp.minimum(j, valid[1] - 1)))],
        out_specs=pl.BlockSpec(block, lambda i, j: (i, j)),
        out_shape=jax.ShapeDtypeStruct(out_shape, BF16),
        compiler_params=pltpu.CompilerParams(
            dimension_semantics=("parallel", "parallel"),
            vmem_limit_bytes=_vmem_limit(_nbytes(block, F32) + _nbytes(block, BF16))),
        name="cast_pad",
    )(w)


def _cast_transposed_kernel(wt_ref, o_ref, *, keep):
    w = wt_ref[...].T
    if keep < w.shape[1]:
        w = jnp.where(lax.broadcasted_iota(jnp.int32, w.shape, 1) < keep, w, 0.0)
    o_ref[...] = w.astype(o_ref.dtype)


def _cast_transposed(w_t, first_row, n_rows, tr=512, keep=None):
    k = w_t.shape[1]
    assert first_row % SUBLANES == 0 and n_rows % tr == 0 and (keep is None or n_rows == tr)
    return pl.pallas_call(
        functools.partial(_cast_transposed_kernel, keep=tr if keep is None else keep),
        grid=(n_rows // tr,),
        in_specs=[pl.BlockSpec((pl.Element(tr), pl.Element(k)), lambda j: (pl.multiple_of(first_row + j * tr, SUBLANES), 0))],
        out_specs=pl.BlockSpec((k, tr), lambda j: (0, j)),
        out_shape=jax.ShapeDtypeStruct((k, n_rows), BF16),
        compiler_params=pltpu.CompilerParams(
            dimension_semantics=("parallel",),
            vmem_limit_bytes=_vmem_limit(_nbytes((tr, k), F32) + _nbytes((k, tr), BF16),
                                         temp_bytes=2 * _nbytes((tr, k), F32))),
        name="cast_transposed",
    )(w_t)


def _ada_kernel(c_ref, w_ref, b_ref, o_ref):
    s = _silu(c_ref[...]).astype(BF16)
    o_ref[...] = jnp.dot(s, w_ref[...].astype(BF16), preferred_element_type=F32) + b_ref[...]


def _ada(cond, w_ada, b_ada):
    n = w_ada.shape[1]
    tn = 512
    return pl.pallas_call(
        _ada_kernel,
        grid=(n // tn,),
        in_specs=[
            pl.BlockSpec((MOD_ROWS, D_MODEL), lambda j: (0, 0)),
            pl.BlockSpec((D_MODEL, tn), lambda j: (0, j)),
            pl.BlockSpec((1, tn), lambda j: (0, j)),
        ],
        out_specs=pl.BlockSpec((MOD_ROWS, tn), lambda j: (0, j)),
        out_shape=jax.ShapeDtypeStruct((MOD_ROWS, n), F32),
        compiler_params=pltpu.CompilerParams(
            dimension_semantics=("arbitrary",),
            vmem_limit_bytes=_vmem_limit(_nbytes((D_MODEL, tn), F32), temp_bytes=_nbytes((D_MODEL, tn), BF16))),
        name="ada",
    )(cond, w_ada, b_ada.reshape(1, n))


def _norm_mod_kernel(x_ref, g_ref, sc_ref, sh_ref, o_ref):
    x = x_ref[...]
    ms = jnp.mean(x * x, axis=-1, keepdims=True)
    h = x * lax.rsqrt(ms + EPS) * g_ref[...]
    o_ref[...] = (h * (1.0 + sc_ref[...]) + sh_ref[...]).astype(o_ref.dtype)


def _rmsnorm_kernel(x_ref, g_ref, o_ref):
    x = x_ref[...]
    ms = jnp.mean(x * x, axis=-1, keepdims=True)
    o_ref[...] = x * lax.rsqrt(ms + EPS) * g_ref[...]


def _mod_spec(which, tn, row_of_tile):
    nb = D_MODEL // tn
    return pl.BlockSpec((None, 1, tn), lambda i, j, *_: (row_of_tile(i), 0, which * nb + j))


def _norm_mod(x, g, mods, which_shift, which_scale, row_of_tile, tm=256):
    m = x.shape[0]
    row = lambda i: row_of_tile(i, tm)
    return pl.pallas_call(
        _norm_mod_kernel,
        grid=(m // tm, 1),
        in_specs=[
            pl.BlockSpec((tm, D_MODEL), lambda i, j: (i, 0)),
            pl.BlockSpec((1, D_MODEL), lambda i, j: (0, 0)),
            _mod_spec(which_scale, D_MODEL, row),
            _mod_spec(which_shift, D_MODEL, row),
        ],
        out_specs=pl.BlockSpec((tm, D_MODEL), lambda i, j: (i, 0)),
        out_shape=jax.ShapeDtypeStruct((m, D_MODEL), BF16),
        compiler_params=pltpu.CompilerParams(
            dimension_semantics=("parallel", "arbitrary"),
            vmem_limit_bytes=_vmem_limit(_nbytes((tm, D_MODEL), F32) + _nbytes((tm, D_MODEL), BF16),
                                         temp_bytes=2 * _nbytes((tm, D_MODEL), F32))),
        name="norm_mod",
    )(x, g.reshape(1, D_MODEL), mods, mods)


def _final_norm(x, g, tm=256):
    m = x.shape[0]
    return pl.pallas_call(
        _rmsnorm_kernel,
        grid=(m // tm,),
        in_specs=[pl.BlockSpec((tm, D_MODEL), lambda i: (i, 0)), pl.BlockSpec((1, D_MODEL), lambda i: (0, 0))],
        out_specs=pl.BlockSpec((tm, D_MODEL), lambda i: (i, 0)),
        out_shape=jax.ShapeDtypeStruct((m, D_MODEL), F32),
        compiler_params=pltpu.CompilerParams(
            dimension_semantics=("parallel",),
            vmem_limit_bytes=_vmem_limit(2 * _nbytes((tm, D_MODEL), F32), temp_bytes=_nbytes((tm, D_MODEL), F32))),
        name="final_norm",
    )(x, g.reshape(1, D_MODEL))


def _mm_kernel(x_ref, w_ref, o_ref):
    o_ref[...] = jnp.dot(x_ref[...], w_ref[...], preferred_element_type=F32).astype(o_ref.dtype)


def _mm(x, w, out_dtype, tm, tn, name):
    m, k = x.shape
    n = w.shape[1]
    blocks = _nbytes((tm, k), BF16) + _nbytes((k, tn), BF16) + _nbytes((tm, tn), out_dtype)
    return pl.pallas_call(
        _mm_kernel,
        grid=(m // tm, n // tn),
        in_specs=[pl.BlockSpec((tm, k), lambda i, j: (i, 0)), pl.BlockSpec((k, tn), lambda i, j: (0, j))],
        out_specs=pl.BlockSpec((tm, tn), lambda i, j: (i, j)),
        out_shape=jax.ShapeDtypeStruct((m, n), out_dtype),
        compiler_params=pltpu.CompilerParams(
            dimension_semantics=("parallel", "parallel"),
            vmem_limit_bytes=_vmem_limit(blocks, temp_bytes=_nbytes((tm, tn), F32))),
        name=name,
    )(x, w)


def _merge_kernel(og_ref, uc_ref, wg_ref, wc_ref, mgg_ref, mgc_ref, o_ref):
    y_gla = jnp.dot(og_ref[...], wg_ref[...], preferred_element_type=F32)
    y_conv = jnp.dot(uc_ref[...], wc_ref[...], preferred_element_type=F32)
    mix = _sigmoid(mgg_ref[...].astype(F32)) * y_gla + _sigmoid(mgc_ref[...].astype(F32)) * y_conv
    o_ref[...] = mix.astype(o_ref.dtype)


def _merge(og, uc, proj, w_gla_o, w_conv_o, tm=1024, tn=512):
    m = og.shape[0]
    blocks = (_nbytes((tm, GLA_DV), BF16) + _nbytes((tm, CONV_W), BF16) + _nbytes((GLA_DV, tn), BF16)
              + _nbytes((CONV_W, tn), BF16) + 3 * _nbytes((tm, tn), BF16))
    return pl.pallas_call(
        _merge_kernel,
        grid=(m // tm, D_MODEL // tn),
        in_specs=[
            pl.BlockSpec((tm, GLA_DV), lambda i, j: (i, 0)),
            pl.BlockSpec((tm, CONV_W), lambda i, j: (i, 0)),
            pl.BlockSpec((GLA_DV, tn), lambda i, j: (0, j)),
            pl.BlockSpec((CONV_W, tn), lambda i, j: (0, j)),
            pl.BlockSpec((tm, tn), lambda i, j: (i, COL_MG_GLA // tn + j)),
            pl.BlockSpec((tm, tn), lambda i, j: (i, COL_MG_CONV // tn + j)),
        ],
        out_specs=pl.BlockSpec((tm, tn), lambda i, j: (i, j)),
        out_shape=jax.ShapeDtypeStruct((m, D_MODEL), BF16),
        compiler_params=pltpu.CompilerParams(
            dimension_semantics=("parallel", "parallel"),
            vmem_limit_bytes=_vmem_limit(blocks, temp_bytes=3 * _nbytes((tm, tn), F32))),
        name="merge",
    )(og, uc, w_gla_o, w_conv_o, proj, proj)


def _mm_resid_kernel(x_ref, w_ref, r_ref, gt_ref, o_ref, *acc, nk):
    part = jnp.dot(x_ref[...], w_ref[...], preferred_element_type=F32)
    if nk == 1:
        o_ref[...] = r_ref[...] + gt_ref[...] * part
        return
    acc_ref, = acc
    kk = pl.program_id(2)

    @pl.when(kk == 0)
    def _():
        acc_ref[...] = part

    @pl.when(jnp.logical_and(kk > 0, kk < nk - 1))
    def _():
        acc_ref[...] += part

    @pl.when(kk == nk - 1)
    def _():
        o_ref[...] = r_ref[...] + gt_ref[...] * (acc_ref[...] + part)


def _mm_resid(x, w, resid, mods, which_gate, row_of_tile, tm, tn, tk, name):
    m, k = x.shape
    n = w.shape[1]
    nk = k // tk
    row = lambda i: row_of_tile(i, tm)
    blocks = _nbytes((tm, tk), BF16) + _nbytes((tk, tn), BF16) + 2 * _nbytes((tm, tn), F32)
    return pl.pallas_call(
        functools.partial(_mm_resid_kernel, nk=nk),
        grid=(m // tm, n // tn, nk),
        in_specs=[
            pl.BlockSpec((tm, tk), lambda i, j, kk: (i, kk)),
            pl.BlockSpec((tk, tn), lambda i, j, kk: (kk, j)),
            pl.BlockSpec((tm, tn), lambda i, j, kk: (i, j)),
            _mod_spec(which_gate, tn, row),
        ],
        out_specs=pl.BlockSpec((tm, tn), lambda i, j, kk: (i, j)),
        out_shape=jax.ShapeDtypeStruct((m, n), F32),
        scratch_shapes=[] if nk == 1 else [pltpu.VMEM((tm, tn), F32)],
        compiler_params=pltpu.CompilerParams(
            dimension_semantics=("parallel", "parallel", "arbitrary"),
            vmem_limit_bytes=_vmem_limit(blocks, scratch_bytes=_nbytes((tm, tn), F32),
                                         temp_bytes=_nbytes((tm, tn), F32))),
        name=name,
    )(x, w, resid, mods)


def _ffn_glu_kernel(x_ref, w1_ref, w3_ref, o_ref):
    x = x_ref[...]
    a = jnp.dot(x, w1_ref[...], preferred_element_type=F32)
    b = jnp.dot(x, w3_ref[...], preferred_element_type=F32)
    o_ref[...] = (_silu(a) * b).astype(o_ref.dtype)


def _ffn_glu(x, w1, w3, tm=1024, tn=512):
    m, k = x.shape
    n = w1.shape[1]
    blocks = _nbytes((tm, k), BF16) + 2 * _nbytes((k, tn), BF16) + _nbytes((tm, tn), BF16)
    return pl.pallas_call(
        _ffn_glu_kernel,
        grid=(m // tm, n // tn),
        in_specs=[
            pl.BlockSpec((tm, k), lambda i, j: (i, 0)),
            pl.BlockSpec((k, tn), lambda i, j: (0, j)),
            pl.BlockSpec((k, tn), lambda i, j: (0, j)),
        ],
        out_specs=pl.BlockSpec((tm, tn), lambda i, j: (i, j)),
        out_shape=jax.ShapeDtypeStruct((m, n), BF16),
        compiler_params=pltpu.CompilerParams(
            dimension_semantics=("parallel", "parallel"),
            vmem_limit_bytes=_vmem_limit(blocks, temp_bytes=3 * _nbytes((tm, tn), F32))),
        name="ffn_glu",
    )(x, w1, w3)


def _split_bf16(x):
    hi = x.astype(BF16)
    lo = (x - hi.astype(F32)).astype(BF16)
    return hi, lo


def _gla_group(q_ref, k_ref, v_ref, a_ref, w_a2, b_a, tri, cross, s_ref, o_ref, t0, head, backward):
    rows = pl.ds(t0, GROUP)
    kcols = pl.ds(head * GLA_HEAD_K, GLA_HEAD_K)
    vcols = pl.ds(head * GLA_HEAD_V, GLA_HEAD_V)
    tn_dims = (((0,), (0,)), ((), ()))
    z = jnp.dot(a_ref[rows, :].astype(BF16), w_a2, preferred_element_type=F32) + b_a
    log_a = (jnp.minimum(z, 0.0) - jnp.log(1.0 + jnp.exp(-jnp.abs(z)))) * (1.0 / GATE_TEMP)
    hi, lo = _split_bf16(log_a)
    yield
    tri_bf = tri.astype(BF16)
    b = jnp.dot(tri_bf, hi, preferred_element_type=F32) + jnp.dot(tri_bf, lo, preferred_element_type=F32)
    yield
    nt_dims = (((1,), (1,)), ((), ()))
    totals = [b[c * CHUNK:c * CHUNK + 1, :] if backward else b[(c + 1) * CHUNK - 1:(c + 1) * CHUNK, :]
              for c in range(CHUNKS_PER_GROUP)]
    decays = [jnp.exp(t) for t in totals]
    scanned_first = lambda c: (c % 2 == 1) if backward else (c % 2 == 0)
    over_chunk = lambda r: jnp.broadcast_to(r, (CHUNK, GLA_HEAD_K))
    one = jnp.ones((CHUNK, GLA_HEAD_K), F32)
    chunks = range(CHUNKS_PER_GROUP)
    b_end = jnp.concatenate([over_chunk(totals[c]) for c in chunks], axis=0)
    q_scale = jnp.concatenate([one if scanned_first(c) else over_chunk(decays[c ^ 1]) for c in chunks], axis=0)
    k_scale = jnp.concatenate([over_chunk(decays[c ^ 1]) if scanned_first(c) else one for c in chunks], axis=0)
    tot_cols = jnp.concatenate(totals + [jnp.zeros((LANES - CHUNKS_PER_GROUP, GLA_HEAD_K), F32)], axis=0).T

    qf = q_ref[rows, kcols].astype(F32) * (GLA_HEAD_K ** -0.5)
    kf = k_ref[rows, kcols].astype(F32)
    vb = v_ref[rows, vcols]
    q_f32 = qf * jnp.exp(b)
    k_f32 = kf * jnp.exp(b_end - b)
    q_dec = q_f32.astype(BF16)
    q_pair = (q_f32 * q_scale).astype(BF16)
    k_dec = (kf * jnp.exp(-b)).astype(BF16)
    k_end = k_f32.astype(BF16)
    k_pair = (k_f32 * k_scale).astype(BF16)
    yield
    within = lax.dot_general(q_dec, k_dec, nt_dims, preferred_element_type=F32)
    across = lax.dot_general(q_dec, k_end, nt_dims, preferred_element_type=F32)
    scores = jnp.where(tri, within, jnp.where(cross, across, 0.0)).astype(BF16)
    o_intra = jnp.dot(scores, vb, preferred_element_type=F32)
    yield

    n_pairs = CHUNKS_PER_GROUP // 2
    for p in (range(n_pairs - 1, -1, -1) if backward else range(n_pairs)):
        pr = slice(2 * p * CHUNK, 2 * (p + 1) * CHUNK)
        decay = jnp.exp(tot_cols[:, 2 * p:2 * p + 1] + tot_cols[:, 2 * p + 1:2 * p + 2])
        s = s_ref[...]
        o_ref[pl.ds(t0 + 2 * p * CHUNK, 2 * CHUNK), vcols] = o_intra[pr, :] + jnp.dot(
            q_pair[pr, :], s.astype(BF16), preferred_element_type=F32)
        s_ref[...] = decay * s + lax.dot_general(k_pair[pr, :], vb[pr, :], tn_dims, preferred_element_type=F32)
        yield


def _gla_kernel(*refs, seq_len, heads, has_init, emit_state):
    (q_ref, k_ref, v_ref, g_ref, a_ref, waf_ref, baf_ref, wab_ref, bab_ref, gn_ref), refs = refs[:10], refs[10:]
    if has_init:
        (s0f_ref, s0b_ref), refs = refs[:2], refs[2:]
    og_ref, refs = refs[0], refs[1:]
    (sf_ref, sb_ref), refs = refs[:2], refs[2:]
    of_ref, ob_ref = refs
    n_groups = seq_len // GROUP

    if has_init:
        sf_ref[...] = s0f_ref[...]
        sb_ref[...] = s0b_ref[...]
    else:
        sf_ref[...] = jnp.zeros_like(sf_ref)
        sb_ref[...] = jnp.zeros_like(sb_ref)

    row = lax.broadcasted_iota(jnp.int32, (GROUP, GROUP), 0)
    col = lax.broadcasted_iota(jnp.int32, (GROUP, GROUP), 1)
    row_chunk, col_chunk = row // CHUNK, col // CHUNK
    same_chunk = row_chunk == col_chunk
    tri_f = jnp.logical_and(same_chunk, row >= col)
    tri_b = jnp.logical_and(same_chunk, col >= row)
    cross_f = jnp.logical_and(row_chunk == col_chunk + 1, col_chunk % 2 == 0)
    cross_b = jnp.logical_and(row_chunk == col_chunk - 1, col_chunk % 2 == 1)

    def fwd(head, group):
        kcols = pl.ds(head * GLA_HEAD_K, GLA_HEAD_K)
        return _gla_group(q_ref, k_ref, v_ref, a_ref, waf_ref[:, kcols], baf_ref[:, kcols], tri_f, cross_f,
                          sf_ref.at[head], of_ref, group * GROUP, head, backward=False)

    def bwd(head, group):
        kcols = pl.ds(head * GLA_HEAD_K, GLA_HEAD_K)
        return _gla_group(q_ref, k_ref, v_ref, a_ref, wab_ref[:, kcols], bab_ref[:, kcols], tri_b, cross_b,
                          sb_ref.at[head], ob_ref, (n_groups - 1 - group) * GROUP, head, backward=True)

    units = [(head, group) for head in range(heads) for group in range(n_groups)]
    f = [fwd(*u) for u in units]
    b = [bwd(*u) for u in units]
    order = [f[0], b[0]] * 4
    for u in range(len(units)):
        if u + 1 < len(units):
            order += [f[u], f[u + 1], b[u], b[u + 1], f[u + 1], b[u + 1]] * 2
        else:
            order += [f[u], b[u]] * 2
    for stage in order:
        next(stage, None)

    def out_step(i, carry):
        t = pl.ds(pl.multiple_of(i * CHUNK, CHUNK), CHUNK)
        for head in range(heads):
            vcols = pl.ds(head * GLA_HEAD_V, GLA_HEAD_V)
            o = of_ref[t, vcols] + ob_ref[t, vcols]
            ms = jnp.mean(o * o, axis=-1, keepdims=True)
            o = o * lax.rsqrt(ms + EPS) * gn_ref[...]
            og_ref[t, vcols] = (o * _silu(g_ref[t, vcols].astype(F32))).astype(og_ref.dtype)
        return carry

    lax.fori_loop(0, seq_len // CHUNK, out_step, 0)


def _gla(proj, a_lr, w_a2f, b_af, w_a2b, b_ab, gn, batch, seq_len, s0=None, emit_state=False):
    n_groups = seq_len // GROUP
    assert GLA_UNITS % n_groups == 0
    heads = min(GLA_UNITS // n_groups, GLA_HEADS)
    wk, wv = heads * GLA_HEAD_K, heads * GLA_HEAD_V
    assert COL_K % wk == 0 and COL_V % wv == 0 and COL_G % wv == 0 and GLA_HEADS % heads == 0
    proj3 = proj.reshape(batch, seq_len, HALF_WIDTH)
    a3 = a_lr.reshape(batch, seq_len, LANES)
    in_specs = [
        pl.BlockSpec((None, seq_len, wk), lambda b, h: (b, 0, COL_Q // wk + h)),
        pl.BlockSpec((None, seq_len, wk), lambda b, h: (b, 0, COL_K // wk + h)),
        pl.BlockSpec((None, seq_len, wv), lambda b, h: (b, 0, COL_V // wv + h)),
        pl.BlockSpec((None, seq_len, wv), lambda b, h: (b, 0, COL_G // wv + h)),
        pl.BlockSpec((None, seq_len, LANES), lambda b, h: (b, 0, 0)),
        pl.BlockSpec((LANES, wk), lambda b, h: (0, h)),
        pl.BlockSpec((1, wk), lambda b, h: (0, h)),
        pl.BlockSpec((LANES, wk), lambda b, h: (0, h)),
        pl.BlockSpec((1, wk), lambda b, h: (0, h)),
        pl.BlockSpec((1, GLA_HEAD_V), lambda b, h: (0, 0)),
    ]
    args = [proj3, proj3, proj3, proj3, a3, w_a2f, b_af, w_a2b, b_ab, gn]
    state_shape = (heads, GLA_HEAD_K, GLA_HEAD_V)
    state_spec = pl.BlockSpec((None,) + state_shape, lambda b, h: (b, h, 0, 0))
    if s0 is not None:
        in_specs += [state_spec, state_spec]
        args += list(s0)
    out_specs = [pl.BlockSpec((None, seq_len, wv), lambda b, h: (b, 0, h))]
    out_shape = [jax.ShapeDtypeStruct((batch, seq_len, GLA_DV), BF16)]
    scratch_shapes = [pltpu.VMEM((seq_len, wv), F32), pltpu.VMEM((seq_len, wv), F32)]
    if emit_state:
        out_specs += [state_spec, state_spec]
        out_shape += [jax.ShapeDtypeStruct((batch, GLA_HEADS, GLA_HEAD_K, GLA_HEAD_V), F32)] * 2
    else:
        scratch_shapes = [pltpu.VMEM(state_shape, F32), pltpu.VMEM(state_shape, F32)] + scratch_shapes
    state_bytes = _nbytes(state_shape, F32)
    blocks = (2 * _nbytes((seq_len, wk), BF16) + 3 * _nbytes((seq_len, wv), BF16)
              + _nbytes((seq_len, LANES), F32) + 2 * state_bytes)
    scratch = 2 * _nbytes((seq_len, wv), F32) + (0 if emit_state else 2 * state_bytes)
    outs = pl.pallas_call(
        functools.partial(_gla_kernel, seq_len=seq_len, heads=heads, has_init=s0 is not None,
                          emit_state=emit_state),
        grid=(batch, GLA_HEADS // heads),
        in_specs=in_specs,
        out_specs=out_specs,
        out_shape=out_shape,
        scratch_shapes=scratch_shapes,
        compiler_params=pltpu.CompilerParams(
            dimension_semantics=("parallel", "arbitrary"),
            vmem_limit_bytes=_vmem_limit(blocks, scratch_bytes=scratch, temp_bytes=8 << 20)),
        name="gla",
    )(*args)
    og = outs[0].reshape(batch * seq_len, GLA_DV)
    return (og, outs[1], outs[2]) if emit_state else (og, None, None)


def _conv_kernel(ua_ref, ug_ref, cw_ref, cb_ref, lng_ref, lnb_ref, o_ref, pad_ref, shift_ref, conv_ref, *, seg_len):
    n_seg = CONV_TILE // seg_len
    stride = seg_len + 2 * CONV_HALO
    pad_rows = n_seg * stride
    rows = min(seg_len, 64)
    glu = ua_ref[...].astype(F32) * _sigmoid(ug_ref[...].astype(F32))
    halo = jnp.zeros((CONV_HALO, CONV_W), F32)
    for s in range(n_seg):
        base = s * stride
        pad_ref[base:base + CONV_HALO, :] = halo
        pad_ref[base + CONV_HALO:base + CONV_HALO + seg_len, :] = glu[s * seg_len:(s + 1) * seg_len, :]
        pad_ref[base + CONV_HALO + seg_len:base + stride, :] = halo
    pad_ref[pad_rows:pad_rows + SUBLANES, :] = jnp.zeros((SUBLANES, CONV_W), F32)
    for r in range(1, SUBLANES):
        shift_ref[r - 1, :, :] = pad_ref[r:pad_rows + r, :]

    def lane_block(cb, carry):
        lanes = pl.ds(pl.multiple_of(cb * LANES, LANES), LANES)
        bias = cb_ref[:, lanes]
        for s in range(n_seg):
            for rb in range(seg_len // rows):
                first = s * stride + CONV_HALO - CONV_K // 2 + rb * rows
                acc = jnp.zeros((rows, LANES), F32)
                for tap in range(CONV_K):
                    aligned, r = divmod(first + tap, SUBLANES)
                    at = pl.ds(aligned * SUBLANES, rows)
                    window = pad_ref[at, lanes] if r == 0 else shift_ref[r - 1, at, lanes]
                    acc = acc + window * cw_ref[tap:tap + 1, lanes]
                conv_ref[pl.ds(s * seg_len + rb * rows, rows), lanes] = acc + bias
        return carry

    lax.fori_loop(0, CONV_W // LANES, lane_block, 0)

    u = conv_ref[...]
    mu = jnp.mean(u, axis=-1, keepdims=True)
    uc = u - mu
    var = jnp.mean(uc * uc, axis=-1, keepdims=True)
    y = uc * lax.rsqrt(var + EPS) * lng_ref[...] + lnb_ref[...]
    o_ref[...] = _silu(y).astype(o_ref.dtype)


def _conv_module(proj, conv_w, conv_b, ln_g, ln_b, seg_len):
    m = proj.shape[0]
    n_seg = CONV_TILE // seg_len
    pad_rows = n_seg * (seg_len + 2 * CONV_HALO)
    row = lambda v: v.reshape(1, CONV_W)
    blocks = 3 * _nbytes((CONV_TILE, CONV_W), BF16)
    scratch = SUBLANES * _nbytes((pad_rows + SUBLANES, CONV_W), F32) + _nbytes((CONV_TILE, CONV_W), F32)
    return pl.pallas_call(
        functools.partial(_conv_kernel, seg_len=seg_len),
        grid=(m // CONV_TILE,),
        in_specs=[
            pl.BlockSpec((CONV_TILE, CONV_W), lambda i: (i, COL_UA // CONV_W)),
            pl.BlockSpec((CONV_TILE, CONV_W), lambda i: (i, COL_UG // CONV_W)),
            pl.BlockSpec((CONV_K, CONV_W), lambda i: (0, 0)),
            pl.BlockSpec((1, CONV_W), lambda i: (0, 0)),
            pl.BlockSpec((1, CONV_W), lambda i: (0, 0)),
            pl.BlockSpec((1, CONV_W), lambda i: (0, 0)),
        ],
        out_specs=pl.BlockSpec((CONV_TILE, CONV_W), lambda i: (i, 0)),
        out_shape=jax.ShapeDtypeStruct((m, CONV_W), BF16),
        scratch_shapes=[pltpu.VMEM((pad_rows + SUBLANES, CONV_W), F32),
                        pltpu.VMEM((SUBLANES - 1, pad_rows, CONV_W), F32),
                        pltpu.VMEM((CONV_TILE, CONV_W), F32)],
        compiler_params=pltpu.CompilerParams(
            dimension_semantics=("parallel",),
            vmem_limit_bytes=_vmem_limit(blocks, scratch_bytes=scratch,
                                         temp_bytes=4 * _nbytes((CONV_TILE, CONV_W), F32))),
        name="conv_module",
    )(proj, proj, conv_w, row(conv_b), row(ln_g), row(ln_b))


def _trunk(x, mods, row_of_tile, batch, seq_len, conv_seg, wts, s0, emit_state):
    h = _norm_mod(x, wts["norm_mix_g"], mods, 0, 1, row_of_tile)
    proj_att = _mm(h, wts["w_att"], BF16, 1024, 1024, "in_proj_att")
    proj_mix = _mm(h, wts["w_mix"], BF16, 1024, 1024, "in_proj_mix")
    a_lr = _mm(h, wts["w_decay"], F32, 1024, LANES, "decay_proj")
    og, s_f, s_b = _gla(proj_att, a_lr, wts["w_a2f"], wts["b_af"], wts["w_a2b"], wts["b_ab"], wts["gla_norm_g"],
                        batch, seq_len, s0=s0, emit_state=emit_state)
    uc = _conv_module(proj_mix, wts["conv_w"], wts["conv_b"], wts["conv_ln_g"], wts["conv_ln_b"], conv_seg)
    mix = _merge(og, uc, proj_mix, wts["w_gla_o"], wts["w_conv_o"])
    x1 = _mm_resid(mix, wts["w_out"], x, mods, 2, row_of_tile, 1024, 1024, D_MODEL, "out_proj")
    h2 = _norm_mod(x1, wts["norm_ffn_g"], mods, 3, 4, row_of_tile)
    hid = _ffn_glu(h2, wts["w_ffn1"], wts["w_ffn3"])
    x2 = _mm_resid(hid, wts["w_ffn2"], x1, mods, 5, row_of_tile, 1024, 1024, FFN_PAD // 4, "ffn_down")
    return _final_norm(x2, wts["final_norm_g"]), s_f, s_b


def kernel(x_prompt, x_sample, state_gla_fwd, state_gla_bwd, c, c_ctx, norm_mix_g, w_ada, b_ada, w_in, w_a2_fwd, b_a_fwd, w_a2_bwd, b_a_bwd, gla_norm_g, w_gla_o, conv_w, conv_b, conv_ln_g, conv_ln_b, w_conv_o, w_out, norm_ffn_g, w_ffn1, w_ffn3, w_ffn2, final_norm_g):
    n_req, seq, _ = x_prompt.shape
    dec_batch, dec_seq, _ = x_sample.shape
    depth = w_in.shape[0]
    assert depth == 1 and dec_batch + 1 <= MOD_ROWS

    w_in_t = jnp.transpose(w_in[0])
    w_att = _cast_transposed(w_in_t, 0, HALF_WIDTH)
    w_mix = _cast_transposed(w_in_t, DECAY_COL0 + 2 * GATE_RANK, HALF_WIDTH)
    w_decay = _cast_transposed(w_in_t, DECAY_COL0, LANES, tr=LANES, keep=2 * GATE_RANK)
    pad_ffn = FFN_PAD - FFN_HIDDEN
    decay_rows = lambda w, first: jnp.pad(w.astype(BF16), ((first, LANES - GATE_RANK - first), (0, 0)))
    wts = {
        "w_att": w_att, "w_mix": w_mix, "w_decay": w_decay,
        "w_a2f": decay_rows(w_a2_fwd[0], 0),
        "w_a2b": decay_rows(w_a2_bwd[0], GATE_RANK),
        "b_af": b_a_fwd[0].reshape(1, GLA_DK),
        "b_ab": b_a_bwd[0].reshape(1, GLA_DK),
        "gla_norm_g": gla_norm_g[0].reshape(1, GLA_HEAD_V),
        "w_gla_o": w_gla_o[0].astype(BF16),
        "conv_w": conv_w[0], "conv_b": conv_b[0], "conv_ln_g": conv_ln_g[0], "conv_ln_b": conv_ln_b[0],
        "w_conv_o": w_conv_o[0].astype(BF16),
        "w_out": w_out[0].astype(BF16),
        "w_ffn1": _cast_pad(w_ffn1[0], (D_MODEL, FFN_PAD), (D_MODEL, pad_ffn)),
        "w_ffn3": _cast_pad(w_ffn3[0], (D_MODEL, FFN_PAD), (D_MODEL, pad_ffn)),
        "w_ffn2": _cast_pad(w_ffn2[0], (FFN_PAD, D_MODEL), (pad_ffn, D_MODEL)),
        "norm_mix_g": norm_mix_g[0], "norm_ffn_g": norm_ffn_g[0], "final_norm_g": final_norm_g,
    }

    cond = jnp.concatenate([c_ctx[None, :], c, jnp.zeros((MOD_ROWS - 1 - dec_batch, D_MODEL), F32)], axis=0)
    mods = _ada(cond, w_ada[0], b_ada[0]).reshape(MOD_ROWS, 1, N_MOD * D_MODEL)

    ctx_row = lambda i, tm: 0
    lat_row = lambda i, tm: 1 + (i * tm) // dec_seq

    y_p, s_f, s_b = _trunk(x_prompt.reshape(n_req * seq, D_MODEL), mods, ctx_row, n_req, seq, seq, wts,
                           None, True)
    y_s, _, _ = _trunk(x_sample.reshape(dec_batch * dec_seq, D_MODEL), mods, lat_row, dec_batch, dec_seq, GRID_W,
                       wts, (state_gla_fwd[:, 0], state_gla_bwd[:, 0]), False)
    state_shape = (n_req, depth, GLA_HEADS, GLA_HEAD_K, GLA_HEAD_V)
    return (y_p.reshape(n_req, seq, D_MODEL), y_s.reshape(dec_batch, dec_seq, D_MODEL),
            s_f.reshape(state_shape), s_b.reshape(state_shape))
```

```python
import functools

import jax
import jax.numpy as jnp
from jax import lax
from jax.experimental import pallas as pl
from jax.experimental.pallas import tpu as pltpu

F32 = jnp.float32
BF16 = jnp.bfloat16

D_MODEL = 4096
GLA_HEADS = 8
GLA_HEAD_K = 256
GLA_HEAD_V = 512
GLA_DK = GLA_HEADS * GLA_HEAD_K
GLA_DV = GLA_HEADS * GLA_HEAD_V
GATE_RANK = 16
GATE_TEMP = 16.0
CHUNK = 64
GRID_W = 64
CONV_W = D_MODEL // 2
CONV_K = 31
FFN_HIDDEN = 11008
N_MOD = 6
EPS = 1e-6

LANES = 128
SUBLANES = 8
VMEM_BYTES_V7X = 64 * 1024 * 1024

COL_Q = 0
COL_K = GLA_DK
COL_V = 2 * GLA_DK
COL_G = COL_V + GLA_DV
HALF_WIDTH = COL_G + GLA_DV
COL_UA = 0
COL_UG = CONV_W
COL_MG_GLA = 2 * CONV_W
COL_MG_CONV = COL_MG_GLA + D_MODEL
assert COL_MG_CONV + D_MODEL == HALF_WIDTH
DECAY_COL0 = HALF_WIDTH
GROUP = 256
CHUNKS_PER_GROUP = GROUP // CHUNK
GLA_UNITS = 8
FFN_PAD = 11264
MOD_ROWS = 8
CONV_TILE = 256
CONV_HALO = 16


def _vmem_limit(block_bytes, scratch_bytes=0, temp_bytes=0):
    return int(min(VMEM_BYTES_V7X - (2 << 20), 2 * block_bytes + scratch_bytes + temp_bytes + (4 << 20)))


def _nbytes(shape, dtype):
    n = 1
    for s in shape:
        n *= s
    return n * jnp.dtype(dtype).itemsize


def _sigmoid(x):
    return jax.nn.sigmoid(x)


def _silu(x):
    return x * jax.nn.sigmoid(x)


def _cast_pad_kernel(x_ref, o_ref, *, valid_blocks):
    inside = jnp.logical_and(pl.program_id(0) < valid_blocks[0], pl.program_id(1) < valid_blocks[1])

    @pl.when(inside)
    def _():
        o_ref[...] = x_ref[...].astype(o_ref.dtype)

    @pl.when(jnp.logical_not(inside))
    def _():
        o_ref[...] = jnp.zeros_like(o_ref)


def _cast_pad(w, out_shape, block):
    assert all(s % b == 0 and o % b == 0 for s, o, b in zip(w.shape, out_shape, block))
    valid = (w.shape[0] // block[0], w.shape[1] // block[1])
    return pl.pallas_call(
        functools.partial(_cast_pad_kernel, valid_blocks=valid),
        grid=(out_shape[0] // block[0], out_shape[1] // block[1]),
        in_specs=[pl.BlockSpec(block, lambda i, j: (jnp.minimum(i, valid[0] - 1), jnp.minimum(j, valid[1] - 1)))],
        out_specs=pl.BlockSpec(block, lambda i, j: (i, j)),
        out_shape=jax.ShapeDtypeStruct(out_shape, BF16),
        compiler_params=pltpu.CompilerParams(
            dimension_semantics=("parallel", "parallel"),
            vmem_limit_bytes=_vmem_limit(_nbytes(block, F32) + _nbytes(block, BF16))),
        name="cast_pad",
    )(w)


def _cast_transposed_kernel(wt_ref, o_ref, *, keep):
    w = wt_ref[...].T
    if keep < w.shape[1]:
        w = jnp.where(lax.broadcasted_iota(jnp.int32, w.shape, 1) < keep, w, 0.0)
    o_ref[...] = w.astype(o_ref.dtype)


def _cast_transposed(w_t, first_row, n_rows, tr=512, keep=None):
    k = w_t.shape[1]
    assert first_row % SUBLANES == 0 and n_rows % tr == 0 and (keep is None or n_rows == tr)
    return pl.pallas_call(
        functools.partial(_cast_transposed_kernel, keep=tr if keep is None else keep),
        grid=(n_rows // tr,),
        in_specs=[pl.BlockSpec((pl.Element(tr), pl.Element(k)), lambda j: (pl.multiple_of(first_row + j * tr, SUBLANES), 0))],
        out_specs=pl.BlockSpec((k, tr), lambda j: (0, j)),
        out_shape=jax.ShapeDtypeStruct((k, n_rows), BF16),
        compiler_params=pltpu.CompilerParams(
            dimension_semantics=("parallel",),
            vmem_limit_bytes=_vmem_limit(_nbytes((tr, k), F32) + _nbytes((k, tr), BF16),
                                         temp_bytes=2 * _nbytes((tr, k), F32))),
        name="cast_transposed",
    )(w_t)


def _ada_kernel(c_ref, w_ref, b_ref, o_ref):
    s = _silu(c_ref[...]).astype(BF16)
    o_ref[...] = jnp.dot(s, w_ref[...].astype(BF16), preferred_element_type=F32) + b_ref[...]


def _ada(cond, w_ada, b_ada):
    n = w_ada.shape[1]
    tn = 512
    return pl.pallas_call(
        _ada_kernel,
        grid=(n // tn,),
        in_specs=[
            pl.BlockSpec((MOD_ROWS, D_MODEL), lambda j: (0, 0)),
            pl.BlockSpec((D_MODEL, tn), lambda j: (0, j)),
            pl.BlockSpec((1, tn), lambda j: (0, j)),
        ],
        out_specs=pl.BlockSpec((MOD_ROWS, tn), lambda j: (0, j)),
        out_shape=jax.ShapeDtypeStruct((MOD_ROWS, n), F32),
        compiler_params=pltpu.CompilerParams(
            dimension_semantics=("arbitrary",),
            vmem_limit_bytes=_vmem_limit(_nbytes((D_MODEL, tn), F32), temp_bytes=_nbytes((D_MODEL, tn), BF16))),
        name="ada",
    )(cond, w_ada, b_ada.reshape(1, n))


def _norm_mod_kernel(x_ref, g_ref, sc_ref, sh_ref, o_ref):
    x = x_ref[...]
    ms = jnp.mean(x * x, axis=-1, keepdims=True)
    h = x * lax.rsqrt(ms + EPS) * g_ref[...]
    o_ref[...] = (h * (1.0 + sc_ref[...]) + sh_ref[...]).astype(o_ref.dtype)


def _rmsnorm_kernel(x_ref, g_ref, o_ref):
    x = x_ref[...]
    ms = jnp.mean(x * x, axis=-1, keepdims=True)
    o_ref[...] = x * lax.rsqrt(ms + EPS) * g_ref[...]


def _mod_spec(which, tn, row_of_tile):
    nb = D_MODEL // tn
    return pl.BlockSpec((None, 1, tn), lambda i, j, *_: (row_of_tile(i), 0, which * nb + j))


def _norm_mod(x, g, mods, which_shift, which_scale, row_of_tile, tm=256):
    m = x.shape[0]
    row = lambda i: row_of_tile(i, tm)
    return pl.pallas_call(
        _norm_mod_kernel,
        grid=(m // tm, 1),
        in_specs=[
            pl.BlockSpec((tm, D_MODEL), lambda i, j: (i, 0)),
            pl.BlockSpec((1, D_MODEL), lambda i, j: (0, 0)),
            _mod_spec(which_scale, D_MODEL, row),
            _mod_spec(which_shift, D_MODEL, row),
        ],
        out_specs=pl.BlockSpec((tm, D_MODEL), lambda i, j: (i, 0)),
        out_shape=jax.ShapeDtypeStruct((m, D_MODEL), BF16),
        compiler_params=pltpu.CompilerParams(
            dimension_semantics=("parallel", "arbitrary"),
            vmem_limit_bytes=_vmem_limit(_nbytes((tm, D_MODEL), F32) + _nbytes((tm, D_MODEL), BF16),
                                         temp_bytes=2 * _nbytes((tm, D_MODEL), F32))),
        name="norm_mod",
    )(x, g.reshape(1, D_MODEL), mods, mods)


def _final_norm(x, g, tm=256):
    m = x.shape[0]
    return pl.pallas_call(
        _rmsnorm_kernel,
        grid=(m // tm,),
        in_specs=[pl.BlockSpec((tm, D_MODEL), lambda i: (i, 0)), pl.BlockSpec((1, D_MODEL), lambda i: (0, 0))],
        out_specs=pl.BlockSpec((tm, D_MODEL), lambda i: (i, 0)),
        out_shape=jax.ShapeDtypeStruct((m, D_MODEL), F32),
        compiler_params=pltpu.CompilerParams(
            dimension_semantics=("parallel",),
            vmem_limit_bytes=_vmem_limit(2 * _nbytes((tm, D_MODEL), F32), temp_bytes=_nbytes((tm, D_MODEL), F32))),
        name="final_norm",
    )(x, g.reshape(1, D_MODEL))


def _mm_kernel(x_ref, w_ref, o_ref):
    o_ref[...] = jnp.dot(x_ref[...], w_ref[...], preferred_element_type=F32).astype(o_ref.dtype)


def _mm(x, w, out_dtype, tm, tn, name):
    m, k = x.shape
    n = w.shape[1]
    blocks = _nbytes((tm, k), BF16) + _nbytes((k, tn), BF16) + _nbytes((tm, tn), out_dtype)
    return pl.pallas_call(
        _mm_kernel,
        grid=(m // tm, n // tn),
        in_specs=[pl.BlockSpec((tm, k), lambda i, j: (i, 0)), pl.BlockSpec((k, tn), lambda i, j: (0, j))],
        out_specs=pl.BlockSpec((tm, tn), lambda i, j: (i, j)),
        out_shape=jax.ShapeDtypeStruct((m, n), out_dtype),
        compiler_params=pltpu.CompilerParams(
            dimension_semantics=("parallel", "parallel"),
            vmem_limit_bytes=_vmem_limit(blocks, temp_bytes=_nbytes((tm, tn), F32))),
        name=name,
    )(x, w)


def _merge_kernel(og_ref, uc_ref, wg_ref, wc_ref, mgg_ref, mgc_ref, o_ref):
    y_gla = jnp.dot(og_ref[...], wg_ref[...], preferred_element_type=F32)
    y_conv = jnp.dot(uc_ref[...], wc_ref[...], preferred_element_type=F32)
    mix = _sigmoid(mgg_ref[...].astype(F32)) * y_gla + _sigmoid(mgc_ref[...].astype(F32)) * y_conv
    o_ref[...] = mix.astype(o_ref.dtype)


def _merge(og, uc, proj, w_gla_o, w_conv_o, tm=1024, tn=512):
    m = og.shape[0]
    blocks = (_nbytes((tm, GLA_DV), BF16) + _nbytes((tm, CONV_W), BF16) + _nbytes((GLA_DV, tn), BF16)
              + _nbytes((CONV_W, tn), BF16) + 3 * _nbytes((tm, tn), BF16))
    return pl.pallas_call(
        _merge_kernel,
        grid=(m // tm, D_MODEL // tn),
        in_specs=[
            pl.BlockSpec((tm, GLA_DV), lambda i, j: (i, 0)),
            pl.BlockSpec((tm, CONV_W), lambda i, j: (i, 0)),
            pl.BlockSpec((GLA_DV, tn), lambda i, j: (0, j)),
            pl.BlockSpec((CONV_W, tn), lambda i, j: (0, j)),
            pl.BlockSpec((tm, tn), lambda i, j: (i, COL_MG_GLA // tn + j)),
            pl.BlockSpec((tm, tn), lambda i, j: (i, COL_MG_CONV // tn + j)),
        ],
        out_specs=pl.BlockSpec((tm, tn), lambda i, j: (i, j)),
        out_shape=jax.ShapeDtypeStruct((m, D_MODEL), BF16),
        compiler_params=pltpu.CompilerParams(
            dimension_semantics=("parallel", "parallel"),
            vmem_limit_bytes=_vmem_limit(blocks, temp_bytes=3 * _nbytes((tm, tn), F32))),
        name="merge",
    )(og, uc, w_gla_o, w_conv_o, proj, proj)


def _mm_resid_kernel(x_ref, w_ref, r_ref, gt_ref, o_ref, *acc, nk):
    part = jnp.dot(x_ref[...], w_ref[...], preferred_element_type=F32)
    if nk == 1:
        o_ref[...] = r_ref[...] + gt_ref[...] * part
        return
    acc_ref, = acc
    kk = pl.program_id(2)

    @pl.when(kk == 0)
    def _():
        acc_ref[...] = part

    @pl.when(jnp.logical_and(kk > 0, kk < nk - 1))
    def _():
        acc_ref[...] += part

    @pl.when(kk == nk - 1)
    def _():
        o_ref[...] = r_ref[...] + gt_ref[...] * (acc_ref[...] + part)


def _mm_resid(x, w, resid, mods, which_gate, row_of_tile, tm, tn, tk, name):
    m, k = x.shape
    n = w.shape[1]
    nk = k // tk
    row = lambda i: row_of_tile(i, tm)
    blocks = _nbytes((tm, tk), BF16) + _nbytes((tk, tn), BF16) + 2 * _nbytes((tm, tn), F32)
    return pl.pallas_call(
        functools.partial(_mm_resid_kernel, nk=nk),
        grid=(m // tm, n // tn, nk),
        in_specs=[
            pl.BlockSpec((tm, tk), lambda i, j, kk: (i, kk)),
            pl.BlockSpec((tk, tn), lambda i, j, kk: (kk, j)),
            pl.BlockSpec((tm, tn), lambda i, j, kk: (i, j)),
            _mod_spec(which_gate, tn, row),
        ],
        out_specs=pl.BlockSpec((tm, tn), lambda i, j, kk: (i, j)),
        out_shape=jax.ShapeDtypeStruct((m, n), F32),
        scratch_shapes=[] if nk == 1 else [pltpu.VMEM((tm, tn), F32)],
        compiler_params=pltpu.CompilerParams(
            dimension_semantics=("parallel", "parallel", "arbitrary"),
            vmem_limit_bytes=_vmem_limit(blocks, scratch_bytes=_nbytes((tm, tn), F32),
                                         temp_bytes=_nbytes((tm, tn), F32))),
        name=name,
    )(x, w, resid, mods)


def _ffn_glu_kernel(x_ref, w1_ref, w3_ref, o_ref):
    x = x_ref[...]
    a = jnp.dot(x, w1_ref[...], preferred_element_type=F32)
    b = jnp.dot(x, w3_ref[...], preferred_element_type=F32)
    o_ref[...] = (_silu(a) * b).astype(o_ref.dtype)


def _ffn_glu(x, w1, w3, tm=1024, tn=512):
    m, k = x.shape
    n = w1.shape[1]
    blocks = _nbytes((tm, k), BF16) + 2 * _nbytes((k, tn), BF16) + _nbytes((tm, tn), BF16)
    return pl.pallas_call(
        _ffn_glu_kernel,
        grid=(m // tm, n // tn),
        in_specs=[
            pl.BlockSpec((tm, k), lambda i, j: (i, 0)),
            pl.BlockSpec((k, tn), lambda i, j: (0, j)),
            pl.BlockSpec((k, tn), lambda i, j: (0, j)),
        ],
        out_specs=pl.BlockSpec((tm, tn), lambda i, j: (i, j)),
        out_shape=jax.ShapeDtypeStruct((m, n), BF16),
        compiler_params=pltpu.CompilerParams(
            dimension_semantics=("parallel", "parallel"),
            vmem_limit_bytes=_vmem_limit(blocks, temp_bytes=3 * _nbytes((tm, tn), F32))),
        name="ffn_glu",
    )(x, w1, w3)


def _split_bf16(x):
    hi = x.astype(BF16)
    lo = (x - hi.astype(F32)).astype(BF16)
    return hi, lo


def _gla_group(q_ref, k_ref, v_ref, a_ref, w_a2, b_a, tri, cross, s_ref, o_ref, t0, head, backward):
    rows = pl.ds(t0, GROUP)
    kcols = pl.ds(head * GLA_HEAD_K, GLA_HEAD_K)
    vcols = pl.ds(head * GLA_HEAD_V, GLA_HEAD_V)
    tn_dims = (((0,), (0,)), ((), ()))
    z = jnp.dot(a_ref[rows, :].astype(BF16), w_a2, preferred_element_type=F32) + b_a
    log_a = (jnp.minimum(z, 0.0) - jnp.log(1.0 + jnp.exp(-jnp.abs(z)))) * (1.0 / GATE_TEMP)
    hi, lo = _split_bf16(log_a)
    yield
    tri_bf = tri.astype(BF16)
    b = jnp.dot(tri_bf, hi, preferred_element_type=F32) + jnp.dot(tri_bf, lo, preferred_element_type=F32)
    yield
    nt_dims = (((1,), (1,)), ((), ()))
    totals = [b[c * CHUNK:c * CHUNK + 1, :] if backward else b[(c + 1) * CHUNK - 1:(c + 1) * CHUNK, :]
              for c in range(CHUNKS_PER_GROUP)]
    decays = [jnp.exp(t) for t in totals]
    scanned_first = lambda c: (c % 2 == 1) if backward else (c % 2 == 0)
    over_chunk = lambda r: jnp.broadcast_to(r, (CHUNK, GLA_HEAD_K))
    one = jnp.ones((CHUNK, GLA_HEAD_K), F32)
    chunks = range(CHUNKS_PER_GROUP)
    b_end = jnp.concatenate([over_chunk(totals[c]) for c in chunks], axis=0)
    q_scale = jnp.concatenate([one if scanned_first(c) else over_chunk(decays[c ^ 1]) for c in chunks], axis=0)
    k_scale = jnp.concatenate([over_chunk(decays[c ^ 1]) if scanned_first(c) else one for c in chunks], axis=0)
    tot_cols = jnp.concatenate(totals + [jnp.zeros((LANES - CHUNKS_PER_GROUP, GLA_HEAD_K), F32)], axis=0).T

    qf = q_ref[rows, kcols].astype(F32) * (GLA_HEAD_K ** -0.5)
    kf = k_ref[rows, kcols].astype(F32)
    vb = v_ref[rows, vcols]
    q_f32 = qf * jnp.exp(b)
    k_f32 = kf * jnp.exp(b_end - b)
    q_dec = q_f32.astype(BF16)
    q_pair = (q_f32 * q_scale).astype(BF16)
    k_dec = (kf * jnp.exp(-b)).astype(BF16)
    k_end = k_f32.astype(BF16)
    k_pair = (k_f32 * k_scale).astype(BF16)
    yield
    within = lax.dot_general(q_dec, k_dec, nt_dims, preferred_element_type=F32)
    across = lax.dot_general(q_dec, k_end, nt_dims, preferred_element_type=F32)
    scores = jnp.where(tri, within, jnp.where(cross, across, 0.0)).astype(BF16)
    o_intra = jnp.dot(scores, vb, preferred_element_type=F32)
    yield

    n_pairs = CHUNKS_PER_GROUP // 2
    for p in (range(n_pairs - 1, -1, -1) if backward else range(n_pairs)):
        pr = slice(2 * p * CHUNK, 2 * (p + 1) * CHUNK)
        decay = jnp.exp(tot_cols[:, 2 * p:2 * p + 1] + tot_cols[:, 2 * p + 1:2 * p + 2])
        s = s_ref[...]
        o_ref[pl.ds(t0 + 2 * p * CHUNK, 2 * CHUNK), vcols] = o_intra[pr, :] + jnp.dot(
            q_pair[pr, :], s.astype(BF16), preferred_element_type=F32)
        s_ref[...] = decay * s + lax.dot_general(k_pair[pr, :], vb[pr, :], tn_dims, preferred_element_type=F32)
        yield


def _gla_kernel(*refs, seq_len, heads, has_init, emit_state):
    (q_ref, k_ref, v_ref, g_ref, a_ref, waf_ref, baf_ref, wab_ref, bab_ref, gn_ref), refs = refs[:10], refs[10:]
    if has_init:
        (s0f_ref, s0b_ref), refs = refs[:2], refs[2:]
    og_ref, refs = refs[0], refs[1:]
    (sf_ref, sb_ref), refs = refs[:2], refs[2:]
    of_ref, ob_ref = refs
    n_groups = seq_len // GROUP

    if has_init:
        sf_ref[...] = s0f_ref[...]
        sb_ref[...] = s0b_ref[...]
    else:
        sf_ref[...] = jnp.zeros_like(sf_ref)
        sb_ref[...] = jnp.zeros_like(sb_ref)

    row = lax.broadcasted_iota(jnp.int32, (GROUP, GROUP), 0)
    col = lax.broadcasted_iota(jnp.int32, (GROUP, GROUP), 1)
    row_chunk, col_chunk = row // CHUNK, col // CHUNK
    same_chunk = row_chunk == col_chunk
    tri_f = jnp.logical_and(same_chunk, row >= col)
    tri_b = jnp.logical_and(same_chunk, col >= row)
    cross_f = jnp.logical_and(row_chunk == col_chunk + 1, col_chunk % 2 == 0)
    cross_b = jnp.logical_and(row_chunk == col_chunk - 1, col_chunk % 2 == 1)

    def fwd(head, group):
        kcols = pl.ds(head * GLA_HEAD_K, GLA_HEAD_K)
        return _gla_group(q_ref, k_ref, v_ref, a_ref, waf_ref[:, kcols], baf_ref[:, kcols], tri_f, cross_f,
                          sf_ref.at[head], of_ref, group * GROUP, head, backward=False)

    def bwd(head, group):
        kcols = pl.ds(head * GLA_HEAD_K, GLA_HEAD_K)
        return _gla_group(q_ref, k_ref, v_ref, a_ref, wab_ref[:, kcols], bab_ref[:, kcols], tri_b, cross_b,
                          sb_ref.at[head], ob_ref, (n_groups - 1 - group) * GROUP, head, backward=True)

    def finish(head, group):
        vcols = pl.ds(head * GLA_HEAD_V, GLA_HEAD_V)
        for c in range(CHUNKS_PER_GROUP):
            t = pl.ds(group * GROUP + c * CHUNK, CHUNK)
            o = of_ref[t, vcols] + ob_ref[t, vcols]
            ms = jnp.mean(o * o, axis=-1, keepdims=True)
            o = o * lax.rsqrt(ms + EPS) * gn_ref[...]
            og_ref[t, vcols] = (o * _silu(g_ref[t, vcols].astype(F32))).astype(og_ref.dtype)
            yield

    units = [(head, group) for head in range(heads) for group in range(n_groups)]
    f = [fwd(*u) for u in units]
    b = [bwd(*u) for u in units]
    finishing = []

    def run(stages):
        for stage in stages:
            next(stage, None)
            if finishing and next(finishing[0], True):
                finishing.pop(0)

    run([f[0], b[0]] * 4)
    for u, (head, group) in enumerate(units):
        if u + 1 < len(units):
            run([f[u], f[u + 1], b[u], b[u + 1], f[u + 1], b[u + 1]] * 2)
        else:
            run([f[u], b[u]] * 2)
        if group >= n_groups - 1 - group:
            finishing += [finish(head, g) for g in sorted({group, n_groups - 1 - group})]
    for stage in finishing:
        for _ in stage:
            pass


def _gla(proj, a_lr, w_a2f, b_af, w_a2b, b_ab, gn, batch, seq_len, s0=None, emit_state=False):
    n_groups = seq_len // GROUP
    assert GLA_UNITS % n_groups == 0
    heads = min(GLA_UNITS // n_groups, GLA_HEADS)
    wk, wv = heads * GLA_HEAD_K, heads * GLA_HEAD_V
    assert COL_K % wk == 0 and COL_V % wv == 0 and COL_G % wv == 0 and GLA_HEADS % heads == 0
    proj3 = proj.reshape(batch, seq_len, HALF_WIDTH)
    a3 = a_lr.reshape(batch, seq_len, LANES)
    in_specs = [
        pl.BlockSpec((None, seq_len, wk), lambda b, h: (b, 0, COL_Q // wk + h)),
        pl.BlockSpec((None, seq_len, wk), lambda b, h: (b, 0, COL_K // wk + h)),
        pl.BlockSpec((None, seq_len, wv), lambda b, h: (b, 0, COL_V // wv + h)),
        pl.BlockSpec((None, seq_len, wv), lambda b, h: (b, 0, COL_G // wv + h)),
        pl.BlockSpec((None, seq_len, LANES), lambda b, h: (b, 0, 0)),
        pl.BlockSpec((LANES, wk), lambda b, h: (0, h)),
        pl.BlockSpec((1, wk), lambda b, h: (0, h)),
        pl.BlockSpec((LANES, wk), lambda b, h: (0, h)),
        pl.BlockSpec((1, wk), lambda b, h: (0, h)),
        pl.BlockSpec((1, GLA_HEAD_V), lambda b, h: (0, 0)),
    ]
    args = [proj3, proj3, proj3, proj3, a3, w_a2f, b_af, w_a2b, b_ab, gn]
    state_shape = (heads, GLA_HEAD_K, GLA_HEAD_V)
    state_spec = pl.BlockSpec((None,) + state_shape, lambda b, h: (b, h, 0, 0))
    if s0 is not None:
        in_specs += [state_spec, state_spec]
        args += list(s0)
    out_specs = [pl.BlockSpec((None, seq_len, wv), lambda b, h: (b, 0, h))]
    out_shape = [jax.ShapeDtypeStruct((batch, seq_len, GLA_DV), BF16)]
    scratch_shapes = [pltpu.VMEM((seq_len, wv), F32), pltpu.VMEM((seq_len, wv), F32)]
    if emit_state:
        out_specs += [state_spec, state_spec]
        out_shape += [jax.ShapeDtypeStruct((batch, GLA_HEADS, GLA_HEAD_K, GLA_HEAD_V), F32)] * 2
    else:
        scratch_shapes = [pltpu.VMEM(state_shape, F32), pltpu.VMEM(state_shape, F32)] + scratch_shapes
    state_bytes = _nbytes(state_shape, F32)
    blocks = (2 * _nbytes((seq_len, wk), BF16) + 3 * _nbytes((seq_len, wv), BF16)
              + _nbytes((seq_len, LANES), F32) + 2 * state_bytes)
    scratch = 2 * _nbytes((seq_len, wv), F32) + (0 if emit_state else 2 * state_bytes)
    outs = pl.pallas_call(
        functools.partial(_gla_kernel, seq_len=seq_len, heads=heads, has_init=s0 is not None,
                          emit_state=emit_state),
        grid=(batch, GLA_HEADS // heads),
        in_specs=in_specs,
        out_specs=out_specs,
        out_shape=out_shape,
        scratch_shapes=scratch_shapes,
        compiler_params=pltpu.CompilerParams(
            dimension_semantics=("parallel", "arbitrary"),
            vmem_limit_bytes=_vmem_limit(blocks, scratch_bytes=scratch, temp_bytes=8 << 20)),
        name="gla",
    )(*args)
    og = outs[0].reshape(batch * seq_len, GLA_DV)
    return (og, outs[1], outs[2]) if emit_state else (og, None, None)


def _conv_kernel(ua_ref, ug_ref, cw_ref, cb_ref, lng_ref, lnb_ref, o_ref, pad_ref, shift_ref, conv_ref, *, seg_len):
    n_seg = CONV_TILE // seg_len
    stride = seg_len + 2 * CONV_HALO
    pad_rows = n_seg * stride
    rows = min(seg_len, 64)
    glu = ua_ref[...].astype(F32) * _sigmoid(ug_ref[...].astype(F32))
    halo = jnp.zeros((CONV_HALO, CONV_W), F32)
    for s in range(n_seg):
        base = s * stride
        pad_ref[base:base + CONV_HALO, :] = halo
        pad_ref[base + CONV_HALO:base + CONV_HALO + seg_len, :] = glu[s * seg_len:(s + 1) * seg_len, :]
        pad_ref[base + CONV_HALO + seg_len:base + stride, :] = halo
    pad_ref[pad_rows:pad_rows + SUBLANES, :] = jnp.zeros((SUBLANES, CONV_W), F32)
    for r in range(1, SUBLANES):
        shift_ref[r - 1, :, :] = pad_ref[r:pad_rows + r, :]

    def lane_block(cb, carry):
        lanes = pl.ds(pl.multiple_of(cb * LANES, LANES), LANES)
        bias = cb_ref[:, lanes]
        for s in range(n_seg):
            for rb in range(seg_len // rows):
                first = s * stride + CONV_HALO - CONV_K // 2 + rb * rows
                acc = jnp.zeros((rows, LANES), F32)
                for tap in range(CONV_K):
                    aligned, r = divmod(first + tap, SUBLANES)
                    at = pl.ds(aligned * SUBLANES, rows)
                    window = pad_ref[at, lanes] if r == 0 else shift_ref[r - 1, at, lanes]
                    acc = acc + window * cw_ref[tap:tap + 1, lanes]
                conv_ref[pl.ds(s * seg_len + rb * rows, rows), lanes] = acc + bias
        return carry

    lax.fori_loop(0, CONV_W // LANES, lane_block, 0)

    u = conv_ref[...]
    mu = jnp.mean(u, axis=-1, keepdims=True)
    uc = u - mu
    var = jnp.mean(uc * uc, axis=-1, keepdims=True)
    y = uc * lax.rsqrt(var + EPS) * lng_ref[...] + lnb_ref[...]
    o_ref[...] = _silu(y).astype(o_ref.dtype)


def _conv_module(proj, conv_w, conv_b, ln_g, ln_b, seg_len):
    m = proj.shape[0]
    n_seg = CONV_TILE // seg_len
    pad_rows = n_seg * (seg_len + 2 * CONV_HALO)
    row = lambda v: v.reshape(1, CONV_W)
    blocks = 3 * _nbytes((CONV_TILE, CONV_W), BF16)
    scratch = SUBLANES * _nbytes((pad_rows + SUBLANES, CONV_W), F32) + _nbytes((CONV_TILE, CONV_W), F32)
    return pl.pallas_call(
        functools.partial(_conv_kernel, seg_len=seg_len),
        grid=(m // CONV_TILE,),
        in_specs=[
            pl.BlockSpec((CONV_TILE, CONV_W), lambda i: (i, COL_UA // CONV_W)),
            pl.BlockSpec((CONV_TILE, CONV_W), lambda i: (i, COL_UG // CONV_W)),
            pl.BlockSpec((CONV_K, CONV_W), lambda i: (0, 0)),
            pl.BlockSpec((1, CONV_W), lambda i: (0, 0)),
            pl.BlockSpec((1, CONV_W), lambda i: (0, 0)),
            pl.BlockSpec((1, CONV_W), lambda i: (0, 0)),
        ],
        out_specs=pl.BlockSpec((CONV_TILE, CONV_W), lambda i: (i, 0)),
        out_shape=jax.ShapeDtypeStruct((m, CONV_W), BF16),
        scratch_shapes=[pltpu.VMEM((pad_rows + SUBLANES, CONV_W), F32),
                        pltpu.VMEM((SUBLANES - 1, pad_rows, CONV_W), F32),
                        pltpu.VMEM((CONV_TILE, CONV_W), F32)],
        compiler_params=pltpu.CompilerParams(
            dimension_semantics=("parallel",),
            vmem_limit_bytes=_vmem_limit(blocks, scratch_bytes=scratch,
                                         temp_bytes=4 * _nbytes((CONV_TILE, CONV_W), F32))),
        name="conv_module",
    )(proj, proj, conv_w, row(conv_b), row(ln_g), row(ln_b))


def _trunk(x, mods, row_of_tile, batch, seq_len, conv_seg, wts, s0, emit_state):
    h = _norm_mod(x, wts["norm_mix_g"], mods, 0, 1, row_of_tile)
    proj_att = _mm(h, wts["w_att"], BF16, 1024, 1024, "in_proj_att")
    proj_mix = _mm(h, wts["w_mix"], BF16, 1024, 1024, "in_proj_mix")
    a_lr = _mm(h, wts["w_decay"], F32, 1024, LANES, "decay_proj")
    og, s_f, s_b = _gla(proj_att, a_lr, wts["w_a2f"], wts["b_af"], wts["w_a2b"], wts["b_ab"], wts["gla_norm_g"],
                        batch, seq_len, s0=s0, emit_state=emit_state)
    uc = _conv_module(proj_mix, wts["conv_w"], wts["conv_b"], wts["conv_ln_g"], wts["conv_ln_b"], conv_seg)
    mix = _merge(og, uc, proj_mix, wts["w_gla_o"], wts["w_conv_o"])
    x1 = _mm_resid(mix, wts["w_out"], x, mods, 2, row_of_tile, 1024, 1024, D_MODEL, "out_proj")
    h2 = _norm_mod(x1, wts["norm_ffn_g"], mods, 3, 4, row_of_tile)
    hid = _ffn_glu(h2, wts["w_ffn1"], wts["w_ffn3"])
    x2 = _mm_resid(hid, wts["w_ffn2"], x1, mods, 5, row_of_tile, 1024, 1024, FFN_PAD // 4, "ffn_down")
    return _final_norm(x2, wts["final_norm_g"]), s_f, s_b


def kernel(x_prompt, x_sample, state_gla_fwd, state_gla_bwd, c, c_ctx, norm_mix_g, w_ada, b_ada, w_in, w_a2_fwd, b_a_fwd, w_a2_bwd, b_a_bwd, gla_norm_g, w_gla_o, conv_w, conv_b, conv_ln_g, conv_ln_b, w_conv_o, w_out, norm_ffn_g, w_ffn1, w_ffn3, w_ffn2, final_norm_g):
    n_req, seq, _ = x_prompt.shape
    dec_batch, dec_seq, _ = x_sample.shape
    depth = w_in.shape[0]
    assert depth == 1 and dec_batch + 1 <= MOD_ROWS

    w_in_t = jnp.transpose(w_in[0])
    w_att = _cast_transposed(w_in_t, 0, HALF_WIDTH)
    w_mix = _cast_transposed(w_in_t, DECAY_COL0 + 2 * GATE_RANK, HALF_WIDTH)
    w_decay = _cast_transposed(w_in_t, DECAY_COL0, LANES, tr=LANES, keep=2 * GATE_RANK)
    pad_ffn = FFN_PAD - FFN_HIDDEN
    decay_rows = lambda w, first: jnp.pad(w.astype(BF16), ((first, LANES - GATE_RANK - first), (0, 0)))
    wts = {
        "w_att": w_att, "w_mix": w_mix, "w_decay": w_decay,
        "w_a2f": decay_rows(w_a2_fwd[0], 0),
        "w_a2b": decay_rows(w_a2_bwd[0], GATE_RANK),
        "b_af": b_a_fwd[0].reshape(1, GLA_DK),
        "b_ab": b_a_bwd[0].reshape(1, GLA_DK),
        "gla_norm_g": gla_norm_g[0].reshape(1, GLA_HEAD_V),
        "w_gla_o": w_gla_o[0].astype(BF16),
        "conv_w": conv_w[0], "conv_b": conv_b[0], "conv_ln_g": conv_ln_g[0], "conv_ln_b": conv_ln_b[0],
        "w_conv_o": w_conv_o[0].astype(BF16),
        "w_out": w_out[0].astype(BF16),
        "w_ffn1": _cast_pad(w_ffn1[0], (D_MODEL, FFN_PAD), (D_MODEL, pad_ffn)),
        "w_ffn3": _cast_pad(w_ffn3[0], (D_MODEL, FFN_PAD), (D_MODEL, pad_ffn)),
        "w_ffn2": _cast_pad(w_ffn2[0], (FFN_PAD, D_MODEL), (pad_ffn, D_MODEL)),
        "norm_mix_g": norm_mix_g[0], "norm_ffn_g": norm_ffn_g[0], "final_norm_g": final_norm_g,
    }

    cond = jnp.concatenate([c_ctx[None, :], c, jnp.zeros((MOD_ROWS - 1 - dec_batch, D_MODEL), F32)], axis=0)
    mods = _ada(cond, w_ada[0], b_ada[0]).reshape(MOD_ROWS, 1, N_MOD * D_MODEL)

    ctx_row = lambda i, tm: 0
    lat_row = lambda i, tm: 1 + (i * tm) // dec_seq

    y_p, s_f, s_b = _trunk(x_prompt.reshape(n_req * seq, D_MODEL), mods, ctx_row, n_req, seq, seq, wts,
                           None, True)
    y_s, _, _ = _trunk(x_sample.reshape(dec_batch * dec_seq, D_MODEL), mods, lat_row, dec_batch, dec_seq, GRID_W,
                       wts, (state_gla_fwd[:, 0], state_gla_bwd[:, 0]), False)
    state_shape = (n_req, depth, GLA_HEADS, GLA_HEAD_K, GLA_HEAD_V)
    return (y_p.reshape(n_req, seq, D_MODEL), y_s.reshape(dec_batch, dec_seq, D_MODEL),
            s_f.reshape(state_shape), s_b.reshape(state_shape))
```

```python
import functools

import jax
import jax.numpy as jnp
from jax import lax
from jax.experimental import pallas as pl
from jax.experimental.pallas import tpu as pltpu

F32 = jnp.float32
BF16 = jnp.bfloat16

D_MODEL = 4096
GLA_HEADS = 8
GLA_HEAD_K = 256
GLA_HEAD_V = 512
GLA_DK = GLA_HEADS * GLA_HEAD_K
GLA_DV = GLA_HEADS * GLA_HEAD_V
GATE_RANK = 16
GATE_TEMP = 16.0
CHUNK = 64
GRID_W = 64
CONV_W = D_MODEL // 2
CONV_K = 31
FFN_HIDDEN = 11008
N_MOD = 6
EPS = 1e-6

LANES = 128
SUBLANES = 8
VMEM_BYTES_V7X = 64 * 1024 * 1024

COL_Q = 0
COL_K = GLA_DK
COL_V = 2 * GLA_DK
COL_G = COL_V + GLA_DV
HALF_WIDTH = COL_G + GLA_DV
COL_UA = 0
COL_UG = CONV_W
COL_MG_GLA = 2 * CONV_W
COL_MG_CONV = COL_MG_GLA + D_MODEL
assert COL_MG_CONV + D_MODEL == HALF_WIDTH
DECAY_COL0 = HALF_WIDTH
GROUP = 256
CHUNKS_PER_GROUP = GROUP // CHUNK
GLA_UNITS = 8
FFN_TAIL = 256
FFN_MAIN = FFN_HIDDEN - FFN_TAIL
MOD_ROWS = 8
CONV_TILE = 256
CONV_HALO = 16


def _vmem_limit(block_bytes, scratch_bytes=0, temp_bytes=0):
    return int(min(VMEM_BYTES_V7X - (2 << 20), 2 * block_bytes + scratch_bytes + temp_bytes + (4 << 20)))


def _nbytes(shape, dtype):
    n = 1
    for s in shape:
        n *= s
    return n * jnp.dtype(dtype).itemsize


def _sigmoid(x):
    return jax.nn.sigmoid(x)


def _silu(x):
    return x * jax.nn.sigmoid(x)


def _cast_kernel(x_ref, o_ref):
    o_ref[...] = x_ref[...].astype(o_ref.dtype)


def _cast(w, block):
    assert all(s % b == 0 for s, b in zip(w.shape, block))
    return pl.pallas_call(
        _cast_kernel,
        grid=(w.shape[0] // block[0], w.shape[1] // block[1]),
        in_specs=[pl.BlockSpec(block, lambda i, j: (i, j))],
        out_specs=pl.BlockSpec(block, lambda i, j: (i, j)),
        out_shape=jax.ShapeDtypeStruct(w.shape, BF16),
        compiler_params=pltpu.CompilerParams(
            dimension_semantics=("parallel", "parallel"),
            vmem_limit_bytes=_vmem_limit(_nbytes(block, F32) + _nbytes(block, BF16))),
        name="cast",
    )(w)


def _cast_transposed_kernel(wt_ref, o_ref, *, keep):
    w = wt_ref[...].T
    if keep < w.shape[1]:
        w = jnp.where(lax.broadcasted_iota(jnp.int32, w.shape, 1) < keep, w, 0.0)
    o_ref[...] = w.astype(o_ref.dtype)


def _cast_transposed(w_t, first_row, n_rows, tr=512, keep=None):
    k = w_t.shape[1]
    assert first_row % SUBLANES == 0 and n_rows % tr == 0 and (keep is None or n_rows == tr)
    return pl.pallas_call(
        functools.partial(_cast_transposed_kernel, keep=tr if keep is None else keep),
        grid=(n_rows // tr,),
        in_specs=[pl.BlockSpec((pl.Element(tr), pl.Element(k)), lambda j: (pl.multiple_of(first_row + j * tr, SUBLANES), 0))],
        out_specs=pl.BlockSpec((k, tr), lambda j: (0, j)),
        out_shape=jax.ShapeDtypeStruct((k, n_rows), BF16),
        compiler_params=pltpu.CompilerParams(
            dimension_semantics=("parallel",),
            vmem_limit_bytes=_vmem_limit(_nbytes((tr, k), F32) + _nbytes((k, tr), BF16),
                                         temp_bytes=2 * _nbytes((tr, k), F32))),
        name="cast_transposed",
    )(w_t)


def _ada_kernel(c_ref, w_ref, b_ref, o_ref):
    s = _silu(c_ref[...]).astype(BF16)
    o_ref[...] = jnp.dot(s, w_ref[...].astype(BF16), preferred_element_type=F32) + b_ref[...]


def _ada(cond, w_ada, b_ada):
    n = w_ada.shape[1]
    tn = 512
    return pl.pallas_call(
        _ada_kernel,
        grid=(n // tn,),
        in_specs=[
            pl.BlockSpec((MOD_ROWS, D_MODEL), lambda j: (0, 0)),
            pl.BlockSpec((D_MODEL, tn), lambda j: (0, j)),
            pl.BlockSpec((1, tn), lambda j: (0, j)),
        ],
        out_specs=pl.BlockSpec((MOD_ROWS, tn), lambda j: (0, j)),
        out_shape=jax.ShapeDtypeStruct((MOD_ROWS, n), F32),
        compiler_params=pltpu.CompilerParams(
            dimension_semantics=("arbitrary",),
            vmem_limit_bytes=_vmem_limit(_nbytes((D_MODEL, tn), F32), temp_bytes=_nbytes((D_MODEL, tn), BF16))),
        name="ada",
    )(cond, w_ada, b_ada.reshape(1, n))


def _norm_mod_kernel(x_ref, g_ref, sc_ref, sh_ref, o_ref):
    x = x_ref[...]
    ms = jnp.mean(x * x, axis=-1, keepdims=True)
    h = x * lax.rsqrt(ms + EPS) * g_ref[...]
    o_ref[...] = (h * (1.0 + sc_ref[...]) + sh_ref[...]).astype(o_ref.dtype)


def _rmsnorm_kernel(x_ref, g_ref, o_ref):
    x = x_ref[...]
    ms = jnp.mean(x * x, axis=-1, keepdims=True)
    o_ref[...] = x * lax.rsqrt(ms + EPS) * g_ref[...]


def _mod_spec(which, tn, row_of_tile):
    nb = D_MODEL // tn
    return pl.BlockSpec((None, 1, tn), lambda i, j, *_: (row_of_tile(i), 0, which * nb + j))


def _norm_mod(x, g, mods, which_shift, which_scale, row_of_tile, tm=256):
    m = x.shape[0]
    row = lambda i: row_of_tile(i, tm)
    return pl.pallas_call(
        _norm_mod_kernel,
        grid=(m // tm, 1),
        in_specs=[
            pl.BlockSpec((tm, D_MODEL), lambda i, j: (i, 0)),
            pl.BlockSpec((1, D_MODEL), lambda i, j: (0, 0)),
            _mod_spec(which_scale, D_MODEL, row),
            _mod_spec(which_shift, D_MODEL, row),
        ],
        out_specs=pl.BlockSpec((tm, D_MODEL), lambda i, j: (i, 0)),
        out_shape=jax.ShapeDtypeStruct((m, D_MODEL), BF16),
        compiler_params=pltpu.CompilerParams(
            dimension_semantics=("parallel", "arbitrary"),
            vmem_limit_bytes=_vmem_limit(_nbytes((tm, D_MODEL), F32) + _nbytes((tm, D_MODEL), BF16),
                                         temp_bytes=2 * _nbytes((tm, D_MODEL), F32))),
        name="norm_mod",
    )(x, g.reshape(1, D_MODEL), mods, mods)


def _final_norm(x, g, tm=256):
    m = x.shape[0]
    return pl.pallas_call(
        _rmsnorm_kernel,
        grid=(m // tm,),
        in_specs=[pl.BlockSpec((tm, D_MODEL), lambda i: (i, 0)), pl.BlockSpec((1, D_MODEL), lambda i: (0, 0))],
        out_specs=pl.BlockSpec((tm, D_MODEL), lambda i: (i, 0)),
        out_shape=jax.ShapeDtypeStruct((m, D_MODEL), F32),
        compiler_params=pltpu.CompilerParams(
            dimension_semantics=("parallel",),
            vmem_limit_bytes=_vmem_limit(2 * _nbytes((tm, D_MODEL), F32), temp_bytes=_nbytes((tm, D_MODEL), F32))),
        name="final_norm",
    )(x, g.reshape(1, D_MODEL))


def _mm_kernel(x_ref, w_ref, o_ref):
    o_ref[...] = jnp.dot(x_ref[...], w_ref[...], preferred_element_type=F32).astype(o_ref.dtype)


def _mm(x, w, out_dtype, tm, tn, name):
    m, k = x.shape
    n = w.shape[1]
    blocks = _nbytes((tm, k), BF16) + _nbytes((k, tn), BF16) + _nbytes((tm, tn), out_dtype)
    return pl.pallas_call(
        _mm_kernel,
        grid=(m // tm, n // tn),
        in_specs=[pl.BlockSpec((tm, k), lambda i, j: (i, 0)), pl.BlockSpec((k, tn), lambda i, j: (0, j))],
        out_specs=pl.BlockSpec((tm, tn), lambda i, j: (i, j)),
        out_shape=jax.ShapeDtypeStruct((m, n), out_dtype),
        compiler_params=pltpu.CompilerParams(
            dimension_semantics=("parallel", "parallel"),
            vmem_limit_bytes=_vmem_limit(blocks, temp_bytes=_nbytes((tm, tn), F32))),
        name=name,
    )(x, w)


def _merge_kernel(og_ref, uc_ref, wg_ref, wc_ref, mgg_ref, mgc_ref, o_ref):
    y_gla = jnp.dot(og_ref[...], wg_ref[...], preferred_element_type=F32)
    y_conv = jnp.dot(uc_ref[...], wc_ref[...], preferred_element_type=F32)
    mix = _sigmoid(mgg_ref[...].astype(F32)) * y_gla + _sigmoid(mgc_ref[...].astype(F32)) * y_conv
    o_ref[...] = mix.astype(o_ref.dtype)


def _merge(og, uc, proj, w_gla_o, w_conv_o, tm=1024, tn=512):
    m = og.shape[0]
    blocks = (_nbytes((tm, GLA_DV), BF16) + _nbytes((tm, CONV_W), BF16) + _nbytes((GLA_DV, tn), BF16)
              + _nbytes((CONV_W, tn), BF16) + 3 * _nbytes((tm, tn), BF16))
    return pl.pallas_call(
        _merge_kernel,
        grid=(m // tm, D_MODEL // tn),
        in_specs=[
            pl.BlockSpec((tm, GLA_DV), lambda i, j: (i, 0)),
            pl.BlockSpec((tm, CONV_W), lambda i, j: (i, 0)),
            pl.BlockSpec((GLA_DV, tn), lambda i, j: (0, j)),
            pl.BlockSpec((CONV_W, tn), lambda i, j: (0, j)),
            pl.BlockSpec((tm, tn), lambda i, j: (i, COL_MG_GLA // tn + j)),
            pl.BlockSpec((tm, tn), lambda i, j: (i, COL_MG_CONV // tn + j)),
        ],
        out_specs=pl.BlockSpec((tm, tn), lambda i, j: (i, j)),
        out_shape=jax.ShapeDtypeStruct((m, D_MODEL), BF16),
        compiler_params=pltpu.CompilerParams(
            dimension_semantics=("parallel", "parallel"),
            vmem_limit_bytes=_vmem_limit(blocks, temp_bytes=3 * _nbytes((tm, tn), F32))),
        name="merge",
    )(og, uc, w_gla_o, w_conv_o, proj, proj)


def _mm_resid_kernel(x_ref, w_ref, *refs, nk, has_tail):
    if has_tail:
        (xt_ref, wt_ref), refs = refs[:2], refs[2:]
    r_ref, gt_ref, o_ref = refs[:3]
    part = jnp.dot(x_ref[...], w_ref[...], preferred_element_type=F32)
    if nk == 1:
        assert not has_tail
        o_ref[...] = r_ref[...] + gt_ref[...] * part
        return
    acc_ref, = refs[3:]
    kk = pl.program_id(2)

    @pl.when(kk == 0)
    def _():
        acc_ref[...] = part

    @pl.when(jnp.logical_and(kk > 0, kk < nk - 1))
    def _():
        acc_ref[...] += part

    @pl.when(kk == nk - 1)
    def _():
        total = acc_ref[...] + part
        if has_tail:
            total = total + jnp.dot(xt_ref[...], wt_ref[...], preferred_element_type=F32)
        o_ref[...] = r_ref[...] + gt_ref[...] * total


def _mm_resid(x, w, resid, mods, which_gate, row_of_tile, tm, tn, tk, name, x_tail=None):
    m, k = x.shape
    n = w.shape[1]
    nk = k // tk
    row = lambda i: row_of_tile(i, tm)
    blocks = _nbytes((tm, tk), BF16) + _nbytes((tk, tn), BF16) + 2 * _nbytes((tm, tn), F32)
    tail_specs, tail_args = [], []
    if x_tail is not None:
        kt = x_tail.shape[1]
        assert k % kt == 0 and k + kt == w.shape[0]
        tail_specs = [pl.BlockSpec((tm, kt), lambda i, j, kk: (i, 0)),
                      pl.BlockSpec((kt, tn), lambda i, j, kk: (k // kt, j))]
        tail_args = [x_tail, w]
        blocks += _nbytes((tm, kt), BF16) + _nbytes((kt, tn), BF16)
    return pl.pallas_call(
        functools.partial(_mm_resid_kernel, nk=nk, has_tail=x_tail is not None),
        grid=(m // tm, n // tn, nk),
        in_specs=[
            pl.BlockSpec((tm, tk), lambda i, j, kk: (i, kk)),
            pl.BlockSpec((tk, tn), lambda i, j, kk: (kk, j)),
            *tail_specs,
            pl.BlockSpec((tm, tn), lambda i, j, kk: (i, j)),
            _mod_spec(which_gate, tn, row),
        ],
        out_specs=pl.BlockSpec((tm, tn), lambda i, j, kk: (i, j)),
        out_shape=jax.ShapeDtypeStruct((m, n), F32),
        scratch_shapes=[] if nk == 1 else [pltpu.VMEM((tm, tn), F32)],
        compiler_params=pltpu.CompilerParams(
            dimension_semantics=("parallel", "parallel", "arbitrary"),
            vmem_limit_bytes=_vmem_limit(blocks, scratch_bytes=_nbytes((tm, tn), F32),
                                         temp_bytes=_nbytes((tm, tn), F32))),
        name=name,
    )(x, w, *tail_args, resid, mods)


def _ffn_glu_kernel(x_ref, w1_ref, w3_ref, o_ref):
    x = x_ref[...]
    a = jnp.dot(x, w1_ref[...], preferred_element_type=F32)
    b = jnp.dot(x, w3_ref[...], preferred_element_type=F32)
    o_ref[...] = (_silu(a) * b).astype(o_ref.dtype)


def _ffn_glu(x, w1, w3, first_col, n, tn, name, tm=1024):
    m, k = x.shape
    assert first_col % tn == 0 and n % tn == 0
    j0 = first_col // tn
    blocks = _nbytes((tm, k), BF16) + 2 * _nbytes((k, tn), BF16) + _nbytes((tm, tn), BF16)
    return pl.pallas_call(
        _ffn_glu_kernel,
        grid=(m // tm, n // tn),
        in_specs=[
            pl.BlockSpec((tm, k), lambda i, j: (i, 0)),
            pl.BlockSpec((k, tn), lambda i, j: (0, j0 + j)),
            pl.BlockSpec((k, tn), lambda i, j: (0, j0 + j)),
        ],
        out_specs=pl.BlockSpec((tm, tn), lambda i, j: (i, j)),
        out_shape=jax.ShapeDtypeStruct((m, n), BF16),
        compiler_params=pltpu.CompilerParams(
            dimension_semantics=("parallel", "parallel"),
            vmem_limit_bytes=_vmem_limit(blocks, temp_bytes=3 * _nbytes((tm, tn), F32))),
        name=name,
    )(x, w1, w3)


def _split_bf16(x):
    hi = x.astype(BF16)
    lo = (x - hi.astype(F32)).astype(BF16)
    return hi, lo


def _gla_group(q_ref, k_ref, v_ref, a_ref, w_a2, b_a, tri, cross, s_ref, o_ref, t0, head, backward):
    rows = pl.ds(t0, GROUP)
    kcols = pl.ds(head * GLA_HEAD_K, GLA_HEAD_K)
    vcols = pl.ds(head * GLA_HEAD_V, GLA_HEAD_V)
    tn_dims = (((0,), (0,)), ((), ()))
    z = jnp.dot(a_ref[rows, :].astype(BF16), w_a2, preferred_element_type=F32) + b_a
    log_a = (jnp.minimum(z, 0.0) - jnp.log(1.0 + jnp.exp(-jnp.abs(z)))) * (1.0 / GATE_TEMP)
    hi, lo = _split_bf16(log_a)
    yield
    tri_bf = tri.astype(BF16)
    b = jnp.dot(tri_bf, hi, preferred_element_type=F32) + jnp.dot(tri_bf, lo, preferred_element_type=F32)
    yield
    nt_dims = (((1,), (1,)), ((), ()))
    totals = [b[c * CHUNK:c * CHUNK + 1, :] if backward else b[(c + 1) * CHUNK - 1:(c + 1) * CHUNK, :]
              for c in range(CHUNKS_PER_GROUP)]
    decays = [jnp.exp(t) for t in totals]
    scanned_first = lambda c: (c % 2 == 1) if backward else (c % 2 == 0)
    over_chunk = lambda r: jnp.broadcast_to(r, (CHUNK, GLA_HEAD_K))
    one = jnp.ones((CHUNK, GLA_HEAD_K), F32)
    chunks = range(CHUNKS_PER_GROUP)
    b_end = jnp.concatenate([over_chunk(totals[c]) for c in chunks], axis=0)
    q_scale = jnp.concatenate([one if scanned_first(c) else over_chunk(decays[c ^ 1]) for c in chunks], axis=0)
    k_scale = jnp.concatenate([over_chunk(decays[c ^ 1]) if scanned_first(c) else one for c in chunks], axis=0)
    tot_cols = jnp.concatenate(totals + [jnp.zeros((LANES - CHUNKS_PER_GROUP, GLA_HEAD_K), F32)], axis=0).T

    qf = q_ref[rows, kcols].astype(F32) * (GLA_HEAD_K ** -0.5)
    kf = k_ref[rows, kcols].astype(F32)
    vb = v_ref[rows, vcols]
    q_f32 = qf * jnp.exp(b)
    k_f32 = kf * jnp.exp(b_end - b)
    q_dec = q_f32.astype(BF16)
    q_pair = (q_f32 * q_scale).astype(BF16)
    k_dec = (kf * jnp.exp(-b)).astype(BF16)
    k_end = k_f32.astype(BF16)
    k_pair = (k_f32 * k_scale).astype(BF16)
    yield
    within = lax.dot_general(q_dec, k_dec, nt_dims, preferred_element_type=F32)
    across = lax.dot_general(q_dec, k_end, nt_dims, preferred_element_type=F32)
    scores = jnp.where(tri, within, jnp.where(cross, across, 0.0)).astype(BF16)
    o_intra = jnp.dot(scores, vb, preferred_element_type=F32)
    yield

    n_pairs = CHUNKS_PER_GROUP // 2
    for p in (range(n_pairs - 1, -1, -1) if backward else range(n_pairs)):
        pr = slice(2 * p * CHUNK, 2 * (p + 1) * CHUNK)
        decay = jnp.exp(tot_cols[:, 2 * p:2 * p + 1] + tot_cols[:, 2 * p + 1:2 * p + 2])
        s = s_ref[...]
        o_ref[pl.ds(t0 + 2 * p * CHUNK, 2 * CHUNK), vcols] = o_intra[pr, :] + jnp.dot(
            q_pair[pr, :], s.astype(BF16), preferred_element_type=F32)
        s_ref[...] = decay * s + lax.dot_general(k_pair[pr, :], vb[pr, :], tn_dims, preferred_element_type=F32)
        yield


def _gla_kernel(*refs, seq_len, heads, has_init, emit_state):
    (q_ref, k_ref, v_ref, g_ref, a_ref, waf_ref, baf_ref, wab_ref, bab_ref, gn_ref), refs = refs[:10], refs[10:]
    if has_init:
        (s0f_ref, s0b_ref), refs = refs[:2], refs[2:]
    og_ref, refs = refs[0], refs[1:]
    (sf_ref, sb_ref), refs = refs[:2], refs[2:]
    of_ref, ob_ref = refs
    n_groups = seq_len // GROUP

    if has_init:
        sf_ref[...] = s0f_ref[...]
        sb_ref[...] = s0b_ref[...]
    else:
        sf_ref[...] = jnp.zeros_like(sf_ref)
        sb_ref[...] = jnp.zeros_like(sb_ref)

    row = lax.broadcasted_iota(jnp.int32, (GROUP, GROUP), 0)
    col = lax.broadcasted_iota(jnp.int32, (GROUP, GROUP), 1)
    row_chunk, col_chunk = row // CHUNK, col // CHUNK
    same_chunk = row_chunk == col_chunk
    tri_f = jnp.logical_and(same_chunk, row >= col)
    tri_b = jnp.logical_and(same_chunk, col >= row)
    cross_f = jnp.logical_and(row_chunk == col_chunk + 1, col_chunk % 2 == 0)
    cross_b = jnp.logical_and(row_chunk == col_chunk - 1, col_chunk % 2 == 1)

    def fwd(head, group):
        kcols = pl.ds(head * GLA_HEAD_K, GLA_HEAD_K)
        return _gla_group(q_ref, k_ref, v_ref, a_ref, waf_ref[:, kcols], baf_ref[:, kcols], tri_f, cross_f,
                          sf_ref.at[head], of_ref, group * GROUP, head, backward=False)

    def bwd(head, group):
        kcols = pl.ds(head * GLA_HEAD_K, GLA_HEAD_K)
        return _gla_group(q_ref, k_ref, v_ref, a_ref, wab_ref[:, kcols], bab_ref[:, kcols], tri_b, cross_b,
                          sb_ref.at[head], ob_ref, (n_groups - 1 - group) * GROUP, head, backward=True)

    def finish(head, group):
        vcols = pl.ds(head * GLA_HEAD_V, GLA_HEAD_V)
        for c in range(CHUNKS_PER_GROUP):
            t = pl.ds(group * GROUP + c * CHUNK, CHUNK)
            o = of_ref[t, vcols] + ob_ref[t, vcols]
            ms = jnp.mean(o * o, axis=-1, keepdims=True)
            o = o * lax.rsqrt(ms + EPS) * gn_ref[...]
            og_ref[t, vcols] = (o * _silu(g_ref[t, vcols].astype(F32))).astype(og_ref.dtype)
            yield

    units = [(head, group) for head in range(heads) for group in range(n_groups)]
    f = [fwd(*u) for u in units]
    b = [bwd(*u) for u in units]
    finishing = []

    def run(stages):
        for stage in stages:
            next(stage, None)
            if finishing and next(finishing[0], True):
                finishing.pop(0)

    run([f[0], b[0]] * 4)
    for u, (head, group) in enumerate(units):
        if u + 1 < len(units):
            run([f[u], f[u + 1], b[u], b[u + 1], f[u + 1], b[u + 1]] * 2)
        else:
            run([f[u], b[u]] * 2)
        if group >= n_groups - 1 - group:
            finishing += [finish(head, g) for g in sorted({group, n_groups - 1 - group})]
    for stage in finishing:
        for _ in stage:
            pass


def _gla(proj, a_lr, w_a2f, b_af, w_a2b, b_ab, gn, batch, seq_len, s0=None, emit_state=False):
    n_groups = seq_len // GROUP
    assert GLA_UNITS % n_groups == 0
    heads = min(GLA_UNITS // n_groups, GLA_HEADS)
    wk, wv = heads * GLA_HEAD_K, heads * GLA_HEAD_V
    assert COL_K % wk == 0 and COL_V % wv == 0 and COL_G % wv == 0 and GLA_HEADS % heads == 0
    proj3 = proj.reshape(batch, seq_len, HALF_WIDTH)
    a3 = a_lr.reshape(batch, seq_len, LANES)
    in_specs = [
        pl.BlockSpec((None, seq_len, wk), lambda b, h: (b, 0, COL_Q // wk + h)),
        pl.BlockSpec((None, seq_len, wk), lambda b, h: (b, 0, COL_K // wk + h)),
        pl.BlockSpec((None, seq_len, wv), lambda b, h: (b, 0, COL_V // wv + h)),
        pl.BlockSpec((None, seq_len, wv), lambda b, h: (b, 0, COL_G // wv + h)),
        pl.BlockSpec((None, seq_len, LANES), lambda b, h: (b, 0, 0)),
        pl.BlockSpec((LANES, wk), lambda b, h: (0, h)),
        pl.BlockSpec((1, wk), lambda b, h: (0, h)),
        pl.BlockSpec((LANES, wk), lambda b, h: (0, h)),
        pl.BlockSpec((1, wk), lambda b, h: (0, h)),
        pl.BlockSpec((1, GLA_HEAD_V), lambda b, h: (0, 0)),
    ]
    args = [proj3, proj3, proj3, proj3, a3, w_a2f, b_af, w_a2b, b_ab, gn]
    state_shape = (heads, GLA_HEAD_K, GLA_HEAD_V)
    state_spec = pl.BlockSpec((None,) + state_shape, lambda b, h: (b, h, 0, 0))
    if s0 is not None:
        in_specs += [state_spec, state_spec]
        args += list(s0)
    out_specs = [pl.BlockSpec((None, seq_len, wv), lambda b, h: (b, 0, h))]
    out_shape = [jax.ShapeDtypeStruct((batch, seq_len, GLA_DV), BF16)]
    scratch_shapes = [pltpu.VMEM((seq_len, wv), F32), pltpu.VMEM((seq_len, wv), F32)]
    if emit_state:
        out_specs += [state_spec, state_spec]
        out_shape += [jax.ShapeDtypeStruct((batch, GLA_HEADS, GLA_HEAD_K, GLA_HEAD_V), F32)] * 2
    else:
        scratch_shapes = [pltpu.VMEM(state_shape, F32), pltpu.VMEM(state_shape, F32)] + scratch_shapes
    state_bytes = _nbytes(state_shape, F32)
    blocks = (2 * _nbytes((seq_len, wk), BF16) + 3 * _nbytes((seq_len, wv), BF16)
              + _nbytes((seq_len, LANES), F32) + 2 * state_bytes)
    scratch = 2 * _nbytes((seq_len, wv), F32) + (0 if emit_state else 2 * state_bytes)
    outs = pl.pallas_call(
        functools.partial(_gla_kernel, seq_len=seq_len, heads=heads, has_init=s0 is not None,
                          emit_state=emit_state),
        grid=(batch, GLA_HEADS // heads),
        in_specs=in_specs,
        out_specs=out_specs,
        out_shape=out_shape,
        scratch_shapes=scratch_shapes,
        compiler_params=pltpu.CompilerParams(
            dimension_semantics=("parallel", "arbitrary"),
            vmem_limit_bytes=_vmem_limit(blocks, scratch_bytes=scratch, temp_bytes=8 << 20)),
        name="gla",
    )(*args)
    og = outs[0].reshape(batch * seq_len, GLA_DV)
    return (og, outs[1], outs[2]) if emit_state else (og, None, None)


def _conv_kernel(ua_ref, ug_ref, cw_ref, cb_ref, lng_ref, lnb_ref, o_ref, pad_ref, shift_ref, conv_ref, *, seg_len):
    n_seg = CONV_TILE // seg_len
    stride = seg_len + 2 * CONV_HALO
    pad_rows = n_seg * stride
    rows = min(seg_len, 64)
    glu = ua_ref[...].astype(F32) * _sigmoid(ug_ref[...].astype(F32))
    halo = jnp.zeros((CONV_HALO, CONV_W), F32)
    for s in range(n_seg):
        base = s * stride
        pad_ref[base:base + CONV_HALO, :] = halo
        pad_ref[base + CONV_HALO:base + CONV_HALO + seg_len, :] = glu[s * seg_len:(s + 1) * seg_len, :]
        pad_ref[base + CONV_HALO + seg_len:base + stride, :] = halo
    pad_ref[pad_rows:pad_rows + SUBLANES, :] = jnp.zeros((SUBLANES, CONV_W), F32)
    for r in range(1, SUBLANES):
        shift_ref[r - 1, :, :] = pad_ref[r:pad_rows + r, :]

    def lane_block(cb, carry):
        lanes = pl.ds(pl.multiple_of(cb * LANES, LANES), LANES)
        bias = cb_ref[:, lanes]
        for s in range(n_seg):
            for rb in range(seg_len // rows):
                first = s * stride + CONV_HALO - CONV_K // 2 + rb * rows
                acc = jnp.zeros((rows, LANES), F32)
                for tap in range(CONV_K):
                    aligned, r = divmod(first + tap, SUBLANES)
                    at = pl.ds(aligned * SUBLANES, rows)
                    window = pad_ref[at, lanes] if r == 0 else shift_ref[r - 1, at, lanes]
                    acc = acc + window * cw_ref[tap:tap + 1, lanes]
                conv_ref[pl.ds(s * seg_len + rb * rows, rows), lanes] = acc + bias
        return carry

    lax.fori_loop(0, CONV_W // LANES, lane_block, 0)

    u = conv_ref[...]
    mu = jnp.mean(u, axis=-1, keepdims=True)
    uc = u - mu
    var = jnp.mean(uc * uc, axis=-1, keepdims=True)
    y = uc * lax.rsqrt(var + EPS) * lng_ref[...] + lnb_ref[...]
    o_ref[...] = _silu(y).astype(o_ref.dtype)


def _conv_module(proj, conv_w, conv_b, ln_g, ln_b, seg_len):
    m = proj.shape[0]
    n_seg = CONV_TILE // seg_len
    pad_rows = n_seg * (seg_len + 2 * CONV_HALO)
    row = lambda v: v.reshape(1, CONV_W)
    blocks = 3 * _nbytes((CONV_TILE, CONV_W), BF16)
    scratch = SUBLANES * _nbytes((pad_rows + SUBLANES, CONV_W), F32) + _nbytes((CONV_TILE, CONV_W), F32)
    return pl.pallas_call(
        functools.partial(_conv_kernel, seg_len=seg_len),
        grid=(m // CONV_TILE,),
        in_specs=[
            pl.BlockSpec((CONV_TILE, CONV_W), lambda i: (i, COL_UA // CONV_W)),
            pl.BlockSpec((CONV_TILE, CONV_W), lambda i: (i, COL_UG // CONV_W)),
            pl.BlockSpec((CONV_K, CONV_W), lambda i: (0, 0)),
            pl.BlockSpec((1, CONV_W), lambda i: (0, 0)),
            pl.BlockSpec((1, CONV_W), lambda i: (0, 0)),
            pl.BlockSpec((1, CONV_W), lambda i: (0, 0)),
        ],
        out_specs=pl.BlockSpec((CONV_TILE, CONV_W), lambda i: (i, 0)),
        out_shape=jax.ShapeDtypeStruct((m, CONV_W), BF16),
        scratch_shapes=[pltpu.VMEM((pad_rows + SUBLANES, CONV_W), F32),
                        pltpu.VMEM((SUBLANES - 1, pad_rows, CONV_W), F32),
                        pltpu.VMEM((CONV_TILE, CONV_W), F32)],
        compiler_params=pltpu.CompilerParams(
            dimension_semantics=("parallel",),
            vmem_limit_bytes=_vmem_limit(blocks, scratch_bytes=scratch,
                                         temp_bytes=4 * _nbytes((CONV_TILE, CONV_W), F32))),
        name="conv_module",
    )(proj, proj, conv_w, row(conv_b), row(ln_g), row(ln_b))


def _trunk(x, mods, row_of_tile, batch, seq_len, conv_seg, wts, s0, emit_state):
    h = _norm_mod(x, wts["norm_mix_g"], mods, 0, 1, row_of_tile)
    proj_att = _mm(h, wts["w_att"], BF16, 1024, 1024, "in_proj_att")
    proj_mix = _mm(h, wts["w_mix"], BF16, 1024, 1024, "in_proj_mix")
    a_lr = _mm(h, wts["w_decay"], F32, 1024, LANES, "decay_proj")
    og, s_f, s_b = _gla(proj_att, a_lr, wts["w_a2f"], wts["b_af"], wts["w_a2b"], wts["b_ab"], wts["gla_norm_g"],
                        batch, seq_len, s0=s0, emit_state=emit_state)
    uc = _conv_module(proj_mix, wts["conv_w"], wts["conv_b"], wts["conv_ln_g"], wts["conv_ln_b"], conv_seg)
    mix = _merge(og, uc, proj_mix, wts["w_gla_o"], wts["w_conv_o"])
    x1 = _mm_resid(mix, wts["w_out"], x, mods, 2, row_of_tile, 1024, 1024, D_MODEL, "out_proj")
    h2 = _norm_mod(x1, wts["norm_ffn_g"], mods, 3, 4, row_of_tile)
    hid = _ffn_glu(h2, wts["w_ffn1"], wts["w_ffn3"], 0, FFN_MAIN, 512, "ffn_glu")
    hid_tail = _ffn_glu(h2, wts["w_ffn1"], wts["w_ffn3"], FFN_MAIN, FFN_TAIL, FFN_TAIL, "ffn_glu_tail")
    x2 = _mm_resid(hid, wts["w_ffn2"], x1, mods, 5, row_of_tile, 1024, 1024, FFN_MAIN // 4, "ffn_down",
                   x_tail=hid_tail)
    return _final_norm(x2, wts["final_norm_g"]), s_f, s_b


def kernel(x_prompt, x_sample, state_gla_fwd, state_gla_bwd, c, c_ctx, norm_mix_g, w_ada, b_ada, w_in, w_a2_fwd, b_a_fwd, w_a2_bwd, b_a_bwd, gla_norm_g, w_gla_o, conv_w, conv_b, conv_ln_g, conv_ln_b, w_conv_o, w_out, norm_ffn_g, w_ffn1, w_ffn3, w_ffn2, final_norm_g):
    n_req, seq, _ = x_prompt.shape
    dec_batch, dec_seq, _ = x_sample.shape
    depth = w_in.shape[0]
    assert depth == 1 and dec_batch + 1 <= MOD_ROWS

    w_in_t = jnp.transpose(w_in[0])
    w_att = _cast_transposed(w_in_t, 0, HALF_WIDTH)
    w_mix = _cast_transposed(w_in_t, DECAY_COL0 + 2 * GATE_RANK, HALF_WIDTH)
    w_decay = _cast_transposed(w_in_t, DECAY_COL0, LANES, tr=LANES, keep=2 * GATE_RANK)
    decay_rows = lambda w, first: jnp.pad(w.astype(BF16), ((first, LANES - GATE_RANK - first), (0, 0)))
    wts = {
        "w_att": w_att, "w_mix": w_mix, "w_decay": w_decay,
        "w_a2f": decay_rows(w_a2_fwd[0], 0),
        "w_a2b": decay_rows(w_a2_bwd[0], GATE_RANK),
        "b_af": b_a_fwd[0].reshape(1, GLA_DK),
        "b_ab": b_a_bwd[0].reshape(1, GLA_DK),
        "gla_norm_g": gla_norm_g[0].reshape(1, GLA_HEAD_V),
        "w_gla_o": w_gla_o[0].astype(BF16),
        "conv_w": conv_w[0], "conv_b": conv_b[0], "conv_ln_g": conv_ln_g[0], "conv_ln_b": conv_ln_b[0],
        "w_conv_o": w_conv_o[0].astype(BF16),
        "w_out": w_out[0].astype(BF16),
        "w_ffn1": _cast(w_ffn1[0], (D_MODEL, FFN_TAIL)),
        "w_ffn3": _cast(w_ffn3[0], (D_MODEL, FFN_TAIL)),
        "w_ffn2": _cast(w_ffn2[0], (FFN_TAIL, D_MODEL)),
        "norm_mix_g": norm_mix_g[0], "norm_ffn_g": norm_ffn_g[0], "final_norm_g": final_norm_g,
    }

    cond = jnp.concatenate([c_ctx[None, :], c, jnp.zeros((MOD_ROWS - 1 - dec_batch, D_MODEL), F32)], axis=0)
    mods = _ada(cond, w_ada[0], b_ada[0]).reshape(MOD_ROWS, 1, N_MOD * D_MODEL)

    ctx_row = lambda i, tm: 0
    lat_row = lambda i, tm: 1 + (i * tm) // dec_seq

    y_p, s_f, s_b = _trunk(x_prompt.reshape(n_req * seq, D_MODEL), mods, ctx_row, n_req, seq, seq, wts,
                           None, True)
    y_s, _, _ = _trunk(x_sample.reshape(dec_batch * dec_seq, D_MODEL), mods, lat_row, dec_batch, dec_seq, GRID_W,
                       wts, (state_gla_fwd[:, 0], state_gla_bwd[:, 0]), False)
    state_shape = (n_req, depth, GLA_HEADS, GLA_HEAD_K, GLA_HEAD_V)
    return (y_p.reshape(n_req, seq, D_MODEL), y_s.reshape(dec_batch, dec_seq, D_MODEL),
            s_f.reshape(state_shape), s_b.reshape(state_shape))
```

```python
import functools

import jax
import jax.numpy as jnp
from jax import lax
from jax.experimental import pallas as pl
from jax.experimental.pallas import tpu as pltpu

F32 = jnp.float32
BF16 = jnp.bfloat16

D_MODEL = 4096
GLA_HEADS = 8
GLA_HEAD_K = 256
GLA_HEAD_V = 512
GLA_DK = GLA_HEADS * GLA_HEAD_K
GLA_DV = GLA_HEADS * GLA_HEAD_V
GATE_RANK = 16
GATE_TEMP = 16.0
CHUNK = 64
GRID_W = 64
CONV_W = D_MODEL // 2
CONV_K = 31
FFN_HIDDEN = 11008
N_MOD = 6
EPS = 1e-6

LANES = 128
SUBLANES = 8
VMEM_BYTES_V7X = 64 * 1024 * 1024

COL_Q = 0
COL_K = GLA_DK
COL_V = 2 * GLA_DK
COL_G = COL_V + GLA_DV
HALF_WIDTH = COL_G + GLA_DV
COL_UA = 0
COL_UG = CONV_W
COL_MG_GLA = 2 * CONV_W
COL_MG_CONV = COL_MG_GLA + D_MODEL
assert COL_MG_CONV + D_MODEL == HALF_WIDTH
DECAY_COL0 = HALF_WIDTH
GROUP = 256
CHUNKS_PER_GROUP = GROUP // CHUNK
GLA_UNITS = 8
FFN_PAD = 11264
MOD_ROWS = 8
CONV_TILE = 256
CONV_HALO = 16


def _vmem_limit(block_bytes, scratch_bytes=0, temp_bytes=0):
    return int(min(VMEM_BYTES_V7X - (2 << 20), 2 * block_bytes + scratch_bytes + temp_bytes + (4 << 20)))


def _nbytes(shape, dtype):
    n = 1
    for s in shape:
        n *= s
    return n * jnp.dtype(dtype).itemsize


def _sigmoid(x):
    return jax.nn.sigmoid(x)


def _silu(x):
    return x * jax.nn.sigmoid(x)


def _cast_pad_kernel(x_ref, o_ref, *, valid_blocks):
    inside = jnp.logical_and(pl.program_id(0) < valid_blocks[0], pl.program_id(1) < valid_blocks[1])

    @pl.when(inside)
    def _():
        o_ref[...] = x_ref[...].astype(o_ref.dtype)

    @pl.when(jnp.logical_not(inside))
    def _():
        o_ref[...] = jnp.zeros_like(o_ref)


def _cast_pad(w, out_shape, block):
    assert all(s % b == 0 and o % b == 0 for s, o, b in zip(w.shape, out_shape, block))
    valid = (w.shape[0] // block[0], w.shape[1] // block[1])
    return pl.pallas_call(
        functools.partial(_cast_pad_kernel, valid_blocks=valid),
        grid=(out_shape[0] // block[0], out_shape[1] // block[1]),
        in_specs=[pl.BlockSpec(block, lambda i, j: (jnp.minimum(i, valid[0] - 1), jnp.minimum(j, valid[1] - 1)))],
        out_specs=pl.BlockSpec(block, lambda i, j: (i, j)),
        out_shape=jax.ShapeDtypeStruct(out_shape, BF16),
        compiler_params=pltpu.CompilerParams(
            dimension_semantics=("parallel", "parallel"),
            vmem_limit_bytes=_vmem_limit(_nbytes(block, F32) + _nbytes(block, BF16))),
        name="cast_pad",
    )(w)


def _cast_transposed_kernel(wt_ref, o_ref, *, keep):
    w = wt_ref[...].T
    if keep < w.shape[1]:
        w = jnp.where(lax.broadcasted_iota(jnp.int32, w.shape, 1) < keep, w, 0.0)
    o_ref[...] = w.astype(o_ref.dtype)


def _cast_transposed(w_t, first_row, n_rows, tr=512, keep=None):
    k = w_t.shape[1]
    assert first_row % SUBLANES == 0 and n_rows % tr == 0 and (keep is None or n_rows == tr)
    return pl.pallas_call(
        functools.partial(_cast_transposed_kernel, keep=tr if keep is None else keep),
        grid=(n_rows // tr,),
        in_specs=[pl.BlockSpec((pl.Element(tr), pl.Element(k)), lambda j: (pl.multiple_of(first_row + j * tr, SUBLANES), 0))],
        out_specs=pl.BlockSpec((k, tr), lambda j: (0, j)),
        out_shape=jax.ShapeDtypeStruct((k, n_rows), BF16),
        compiler_params=pltpu.CompilerParams(
            dimension_semantics=("parallel",),
            vmem_limit_bytes=_vmem_limit(_nbytes((tr, k), F32) + _nbytes((k, tr), BF16),
                                         temp_bytes=2 * _nbytes((tr, k), F32))),
        name="cast_transposed",
    )(w_t)


def _ada_kernel(c_ref, w_ref, b_ref, o_ref):
    s = _silu(c_ref[...]).astype(BF16)
    o_ref[...] = jnp.dot(s, w_ref[...].astype(BF16), preferred_element_type=F32) + b_ref[...]


def _ada(cond, w_ada, b_ada):
    n = w_ada.shape[1]
    tn = 512
    return pl.pallas_call(
        _ada_kernel,
        grid=(n // tn,),
        in_specs=[
            pl.BlockSpec((MOD_ROWS, D_MODEL), lambda j: (0, 0)),
            pl.BlockSpec((D_MODEL, tn), lambda j: (0, j)),
            pl.BlockSpec((1, tn), lambda j: (0, j)),
        ],
        out_specs=pl.BlockSpec((MOD_ROWS, tn), lambda j: (0, j)),
        out_shape=jax.ShapeDtypeStruct((MOD_ROWS, n), F32),
        compiler_params=pltpu.CompilerParams(
            dimension_semantics=("arbitrary",),
            vmem_limit_bytes=_vmem_limit(_nbytes((D_MODEL, tn), F32), temp_bytes=_nbytes((D_MODEL, tn), BF16))),
        name="ada",
    )(cond, w_ada, b_ada.reshape(1, n))


def _norm_mod_kernel(x_ref, g_ref, sc_ref, sh_ref, o_ref):
    x = x_ref[...]
    ms = jnp.mean(x * x, axis=-1, keepdims=True)
    h = x * lax.rsqrt(ms + EPS) * g_ref[...]
    o_ref[...] = (h * (1.0 + sc_ref[...]) + sh_ref[...]).astype(o_ref.dtype)


def _rmsnorm_kernel(x_ref, g_ref, o_ref):
    x = x_ref[...]
    ms = jnp.mean(x * x, axis=-1, keepdims=True)
    o_ref[...] = x * lax.rsqrt(ms + EPS) * g_ref[...]


def _mod_spec(which, tn, row_of_tile):
    nb = D_MODEL // tn
    return pl.BlockSpec((None, 1, tn), lambda i, j, *_: (row_of_tile(i), 0, which * nb + j))


def _norm_mod(x, g, mods, which_shift, which_scale, row_of_tile, tm=256):
    m = x.shape[0]
    row = lambda i: row_of_tile(i, tm)
    return pl.pallas_call(
        _norm_mod_kernel,
        grid=(m // tm, 1),
        in_specs=[
            pl.BlockSpec((tm, D_MODEL), lambda i, j: (i, 0)),
            pl.BlockSpec((1, D_MODEL), lambda i, j: (0, 0)),
            _mod_spec(which_scale, D_MODEL, row),
            _mod_spec(which_shift, D_MODEL, row),
        ],
        out_specs=pl.BlockSpec((tm, D_MODEL), lambda i, j: (i, 0)),
        out_shape=jax.ShapeDtypeStruct((m, D_MODEL), BF16),
        compiler_params=pltpu.CompilerParams(
            dimension_semantics=("parallel", "arbitrary"),
            vmem_limit_bytes=_vmem_limit(_nbytes((tm, D_MODEL), F32) + _nbytes((tm, D_MODEL), BF16),
                                         temp_bytes=2 * _nbytes((tm, D_MODEL), F32))),
        name="norm_mod",
    )(x, g.reshape(1, D_MODEL), mods, mods)


def _final_norm(x, g, tm=256):
    m = x.shape[0]
    return pl.pallas_call(
        _rmsnorm_kernel,
        grid=(m // tm,),
        in_specs=[pl.BlockSpec((tm, D_MODEL), lambda i: (i, 0)), pl.BlockSpec((1, D_MODEL), lambda i: (0, 0))],
        out_specs=pl.BlockSpec((tm, D_MODEL), lambda i: (i, 0)),
        out_shape=jax.ShapeDtypeStruct((m, D_MODEL), F32),
        compiler_params=pltpu.CompilerParams(
            dimension_semantics=("parallel",),
            vmem_limit_bytes=_vmem_limit(2 * _nbytes((tm, D_MODEL), F32), temp_bytes=_nbytes((tm, D_MODEL), F32))),
        name="final_norm",
    )(x, g.reshape(1, D_MODEL))


def _mm_kernel(x_ref, w_ref, o_ref):
    o_ref[...] = jnp.dot(x_ref[...], w_ref[...], preferred_element_type=F32).astype(o_ref.dtype)


def _mm(x, w, out_dtype, tm, tn, name):
    m, k = x.shape
    n = w.shape[1]
    blocks = _nbytes((tm, k), BF16) + _nbytes((k, tn), BF16) + _nbytes((tm, tn), out_dtype)
    return pl.pallas_call(
        _mm_kernel,
        grid=(m // tm, n // tn),
        in_specs=[pl.BlockSpec((tm, k), lambda i, j: (i, 0)), pl.BlockSpec((k, tn), lambda i, j: (0, j))],
        out_specs=pl.BlockSpec((tm, tn), lambda i, j: (i, j)),
        out_shape=jax.ShapeDtypeStruct((m, n), out_dtype),
        compiler_params=pltpu.CompilerParams(
            dimension_semantics=("parallel", "parallel"),
            vmem_limit_bytes=_vmem_limit(blocks, temp_bytes=_nbytes((tm, tn), F32))),
        name=name,
    )(x, w)


def _merge_kernel(og_ref, uc_ref, wg_ref, wc_ref, mgg_ref, mgc_ref, o_ref):
    y_gla = jnp.dot(og_ref[...], wg_ref[...], preferred_element_type=F32)
    y_conv = jnp.dot(uc_ref[...], wc_ref[...], preferred_element_type=F32)
    mix = _sigmoid(mgg_ref[...].astype(F32)) * y_gla + _sigmoid(mgc_ref[...].astype(F32)) * y_conv
    o_ref[...] = mix.astype(o_ref.dtype)


def _merge(og, uc, proj, w_gla_o, w_conv_o, tm=1024, tn=512):
    m = og.shape[0]
    blocks = (_nbytes((tm, GLA_DV), BF16) + _nbytes((tm, CONV_W), BF16) + _nbytes((GLA_DV, tn), BF16)
              + _nbytes((CONV_W, tn), BF16) + 3 * _nbytes((tm, tn), BF16))
    return pl.pallas_call(
        _merge_kernel,
        grid=(m // tm, D_MODEL // tn),
        in_specs=[
            pl.BlockSpec((tm, GLA_DV), lambda i, j: (i, 0)),
            pl.BlockSpec((tm, CONV_W), lambda i, j: (i, 0)),
            pl.BlockSpec((GLA_DV, tn), lambda i, j: (0, j)),
            pl.BlockSpec((CONV_W, tn), lambda i, j: (0, j)),
            pl.BlockSpec((tm, tn), lambda i, j: (i, COL_MG_GLA // tn + j)),
            pl.BlockSpec((tm, tn), lambda i, j: (i, COL_MG_CONV // tn + j)),
        ],
        out_specs=pl.BlockSpec((tm, tn), lambda i, j: (i, j)),
        out_shape=jax.ShapeDtypeStruct((m, D_MODEL), BF16),
        compiler_params=pltpu.CompilerParams(
            dimension_semantics=("parallel", "parallel"),
            vmem_limit_bytes=_vmem_limit(blocks, temp_bytes=3 * _nbytes((tm, tn), F32))),
        name="merge",
    )(og, uc, w_gla_o, w_conv_o, proj, proj)


def _mm_resid_kernel(x_ref, w_ref, r_ref, gt_ref, o_ref, *acc, nk):
    part = jnp.dot(x_ref[...], w_ref[...], preferred_element_type=F32)
    if nk == 1:
        o_ref[...] = r_ref[...] + gt_ref[...] * part
        return
    acc_ref, = acc
    kk = pl.program_id(2)

    @pl.when(kk == 0)
    def _():
        acc_ref[...] = part

    @pl.when(jnp.logical_and(kk > 0, kk < nk - 1))
    def _():
        acc_ref[...] += part

    @pl.when(kk == nk - 1)
    def _():
        o_ref[...] = r_ref[...] + gt_ref[...] * (acc_ref[...] + part)


def _mm_resid(x, w, resid, mods, which_gate, row_of_tile, tm, tn, tk, name):
    m, k = x.shape
    n = w.shape[1]
    nk = k // tk
    row = lambda i: row_of_tile(i, tm)
    blocks = _nbytes((tm, tk), BF16) + _nbytes((tk, tn), BF16) + 2 * _nbytes((tm, tn), F32)
    return pl.pallas_call(
        functools.partial(_mm_resid_kernel, nk=nk),
        grid=(m // tm, n // tn, nk),
        in_specs=[
            pl.BlockSpec((tm, tk), lambda i, j, kk: (i, kk)),
            pl.BlockSpec((tk, tn), lambda i, j, kk: (kk, j)),
            pl.BlockSpec((tm, tn), lambda i, j, kk: (i, j)),
            _mod_spec(which_gate, tn, row),
        ],
        out_specs=pl.BlockSpec((tm, tn), lambda i, j, kk: (i, j)),
        out_shape=jax.ShapeDtypeStruct((m, n), F32),
        scratch_shapes=[] if nk == 1 else [pltpu.VMEM((tm, tn), F32)],
        compiler_params=pltpu.CompilerParams(
            dimension_semantics=("parallel", "parallel", "arbitrary"),
            vmem_limit_bytes=_vmem_limit(blocks, scratch_bytes=_nbytes((tm, tn), F32),
                                         temp_bytes=_nbytes((tm, tn), F32))),
        name=name,
    )(x, w, resid, mods)


def _ffn_glu_kernel(x_ref, w1_ref, w3_ref, o_ref):
    x = x_ref[...]
    a = jnp.dot(x, w1_ref[...], preferred_element_type=F32)
    b = jnp.dot(x, w3_ref[...], preferred_element_type=F32)
    o_ref[...] = (_silu(a) * b).astype(o_ref.dtype)


def _ffn_glu(x, w1, w3, tm=1024, tn=512):
    m, k = x.shape
    n = w1.shape[1]
    blocks = _nbytes((tm, k), BF16) + 2 * _nbytes((k, tn), BF16) + _nbytes((tm, tn), BF16)
    return pl.pallas_call(
        _ffn_glu_kernel,
        grid=(m // tm, n // tn),
        in_specs=[
            pl.BlockSpec((tm, k), lambda i, j: (i, 0)),
            pl.BlockSpec((k, tn), lambda i, j: (0, j)),
            pl.BlockSpec((k, tn), lambda i, j: (0, j)),
        ],
        out_specs=pl.BlockSpec((tm, tn), lambda i, j: (i, j)),
        out_shape=jax.ShapeDtypeStruct((m, n), BF16),
        compiler_params=pltpu.CompilerParams(
            dimension_semantics=("parallel", "parallel"),
            vmem_limit_bytes=_vmem_limit(blocks, temp_bytes=3 * _nbytes((tm, tn), F32))),
        name="ffn_glu",
    )(x, w1, w3)


def _split_bf16(x):
    hi = x.astype(BF16)
    lo = (x - hi.astype(F32)).astype(BF16)
    return hi, lo


def _gla_group(q_ref, k_ref, v_ref, a_ref, w_a2, b_a, tri, cross, s_ref, o_ref, t0, head, backward):
    rows = pl.ds(t0, GROUP)
    kcols = pl.ds(head * GLA_HEAD_K, GLA_HEAD_K)
    vcols = pl.ds(head * GLA_HEAD_V, GLA_HEAD_V)
    tn_dims = (((0,), (0,)), ((), ()))
    z = jnp.dot(a_ref[rows, :].astype(BF16), w_a2, preferred_element_type=F32) + b_a
    log_a = (jnp.minimum(z, 0.0) - jnp.log(1.0 + jnp.exp(-jnp.abs(z)))) * (1.0 / GATE_TEMP)
    hi, lo = _split_bf16(log_a)
    yield
    tri_bf = tri.astype(BF16)
    b = jnp.dot(tri_bf, hi, preferred_element_type=F32) + jnp.dot(tri_bf, lo, preferred_element_type=F32)
    yield
    nt_dims = (((1,), (1,)), ((), ()))
    totals = [b[c * CHUNK:c * CHUNK + 1, :] if backward else b[(c + 1) * CHUNK - 1:(c + 1) * CHUNK, :]
              for c in range(CHUNKS_PER_GROUP)]
    decays = [jnp.exp(t) for t in totals]
    scanned_first = lambda c: (c % 2 == 1) if backward else (c % 2 == 0)
    over_chunk = lambda r: jnp.broadcast_to(r, (CHUNK, GLA_HEAD_K))
    one = jnp.ones((CHUNK, GLA_HEAD_K), F32)
    chunks = range(CHUNKS_PER_GROUP)
    b_end = jnp.concatenate([over_chunk(totals[c]) for c in chunks], axis=0)
    q_scale = jnp.concatenate([one if scanned_first(c) else over_chunk(decays[c ^ 1]) for c in chunks], axis=0)
    k_scale = jnp.concatenate([over_chunk(decays[c ^ 1]) if scanned_first(c) else one for c in chunks], axis=0)
    tot_cols = jnp.concatenate(totals + [jnp.zeros((LANES - CHUNKS_PER_GROUP, GLA_HEAD_K), F32)], axis=0).T

    qf = q_ref[rows, kcols].astype(F32) * (GLA_HEAD_K ** -0.5)
    kf = k_ref[rows, kcols].astype(F32)
    vb = v_ref[rows, vcols]
    q_f32 = qf * jnp.exp(b)
    k_f32 = kf * jnp.exp(b_end - b)
    q_dec = q_f32.astype(BF16)
    q_pair = (q_f32 * q_scale).astype(BF16)
    k_dec = (kf * jnp.exp(-b)).astype(BF16)
    k_end = k_f32.astype(BF16)
    k_pair = (k_f32 * k_scale).astype(BF16)
    yield
    within = lax.dot_general(q_dec, k_dec, nt_dims, preferred_element_type=F32)
    across = lax.dot_general(q_dec, k_end, nt_dims, preferred_element_type=F32)
    scores = jnp.where(tri, within, jnp.where(cross, across, 0.0)).astype(BF16)
    o_intra = jnp.dot(scores, vb, preferred_element_type=F32)
    yield

    n_pairs = CHUNKS_PER_GROUP // 2
    for p in (range(n_pairs - 1, -1, -1) if backward else range(n_pairs)):
        pr = slice(2 * p * CHUNK, 2 * (p + 1) * CHUNK)
        decay = jnp.exp(tot_cols[:, 2 * p:2 * p + 1] + tot_cols[:, 2 * p + 1:2 * p + 2])
        s = s_ref[...]
        o_ref[pl.ds(t0 + 2 * p * CHUNK, 2 * CHUNK), vcols] = o_intra[pr, :] + jnp.dot(
            q_pair[pr, :], s.astype(BF16), preferred_element_type=F32)
        s_ref[...] = decay * s + lax.dot_general(k_pair[pr, :], vb[pr, :], tn_dims, preferred_element_type=F32)
        yield


def _gla_kernel(*refs, seq_len, heads, has_init, emit_state):
    (q_ref, k_ref, v_ref, g_ref, a_ref, waf_ref, baf_ref, wab_ref, bab_ref, gn_ref), refs = refs[:10], refs[10:]
    if has_init:
        (s0f_ref, s0b_ref), refs = refs[:2], refs[2:]
    og_ref, refs = refs[0], refs[1:]
    (sf_ref, sb_ref), refs = refs[:2], refs[2:]
    of_ref, ob_ref = refs
    n_groups = seq_len // GROUP

    if has_init:
        sf_ref[...] = s0f_ref[...]
        sb_ref[...] = s0b_ref[...]
    else:
        sf_ref[...] = jnp.zeros_like(sf_ref)
        sb_ref[...] = jnp.zeros_like(sb_ref)

    row = lax.broadcasted_iota(jnp.int32, (GROUP, GROUP), 0)
    col = lax.broadcasted_iota(jnp.int32, (GROUP, GROUP), 1)
    row_chunk, col_chunk = row // CHUNK, col // CHUNK
    same_chunk = row_chunk == col_chunk
    tri_f = jnp.logical_and(same_chunk, row >= col)
    tri_b = jnp.logical_and(same_chunk, col >= row)
    cross_f = jnp.logical_and(row_chunk == col_chunk + 1, col_chunk % 2 == 0)
    cross_b = jnp.logical_and(row_chunk == col_chunk - 1, col_chunk % 2 == 1)

    def fwd(head, group):
        kcols = pl.ds(head * GLA_HEAD_K, GLA_HEAD_K)
        return _gla_group(q_ref, k_ref, v_ref, a_ref, waf_ref[:, kcols], baf_ref[:, kcols], tri_f, cross_f,
                          sf_ref.at[head], of_ref, group * GROUP, head, backward=False)

    def bwd(head, group):
        kcols = pl.ds(head * GLA_HEAD_K, GLA_HEAD_K)
        return _gla_group(q_ref, k_ref, v_ref, a_ref, wab_ref[:, kcols], bab_ref[:, kcols], tri_b, cross_b,
                          sb_ref.at[head], ob_ref, (n_groups - 1 - group) * GROUP, head, backward=True)

    def finish(head, group):
        vcols = pl.ds(head * GLA_HEAD_V, GLA_HEAD_V)
        for c in range(CHUNKS_PER_GROUP):
            t = pl.ds(group * GROUP + c * CHUNK, CHUNK)
            o = of_ref[t, vcols] + ob_ref[t, vcols]
            ms = jnp.mean(o * o, axis=-1, keepdims=True)
            o = o * lax.rsqrt(ms + EPS) * gn_ref[...]
            og_ref[t, vcols] = (o * _silu(g_ref[t, vcols].astype(F32))).astype(og_ref.dtype)
            yield

    units = [(head, group) for head in range(heads) for group in range(n_groups)]
    f = [fwd(*u) for u in units]
    b = [bwd(*u) for u in units]
    finishing = []

    def run(stages):
        for stage in stages:
            next(stage, None)
            if finishing and next(finishing[0], True):
                finishing.pop(0)

    run([f[0], b[0]] * 4)
    for u, (head, group) in enumerate(units):
        if u + 1 < len(units):
            run([f[u], f[u + 1], b[u], b[u + 1], f[u + 1], b[u + 1]] * 2)
        else:
            run([f[u], b[u]] * 2)
        if group >= n_groups - 1 - group:
            finishing += [finish(head, g) for g in sorted({group, n_groups - 1 - group})]
    for stage in finishing:
        for _ in stage:
            pass


def _gla(proj, a_lr, w_a2f, b_af, w_a2b, b_ab, gn, batch, seq_len, s0=None, emit_state=False):
    n_groups = seq_len // GROUP
    assert GLA_UNITS % n_groups == 0
    heads = min(GLA_UNITS // n_groups, GLA_HEADS)
    wk, wv = heads * GLA_HEAD_K, heads * GLA_HEAD_V
    assert COL_K % wk == 0 and COL_V % wv == 0 and COL_G % wv == 0 and GLA_HEADS % heads == 0
    proj3 = proj.reshape(batch, seq_len, HALF_WIDTH)
    a3 = a_lr.reshape(batch, seq_len, LANES)
    in_specs = [
        pl.BlockSpec((None, seq_len, wk), lambda b, h: (b, 0, COL_Q // wk + h)),
        pl.BlockSpec((None, seq_len, wk), lambda b, h: (b, 0, COL_K // wk + h)),
        pl.BlockSpec((None, seq_len, wv), lambda b, h: (b, 0, COL_V // wv + h)),
        pl.BlockSpec((None, seq_len, wv), lambda b, h: (b, 0, COL_G // wv + h)),
        pl.BlockSpec((None, seq_len, LANES), lambda b, h: (b, 0, 0)),
        pl.BlockSpec((LANES, wk), lambda b, h: (0, h)),
        pl.BlockSpec((1, wk), lambda b, h: (0, h)),
        pl.BlockSpec((LANES, wk), lambda b, h: (0, h)),
        pl.BlockSpec((1, wk), lambda b, h: (0, h)),
        pl.BlockSpec((1, GLA_HEAD_V), lambda b, h: (0, 0)),
    ]
    args = [proj3, proj3, proj3, proj3, a3, w_a2f, b_af, w_a2b, b_ab, gn]
    state_shape = (heads, GLA_HEAD_K, GLA_HEAD_V)
    state_spec = pl.BlockSpec((None,) + state_shape, lambda b, h: (b, h, 0, 0))
    if s0 is not None:
        in_specs += [state_spec, state_spec]
        args += list(s0)
    out_specs = [pl.BlockSpec((None, seq_len, wv), lambda b, h: (b, 0, h))]
    out_shape = [jax.ShapeDtypeStruct((batch, seq_len, GLA_DV), BF16)]
    scratch_shapes = [pltpu.VMEM((seq_len, wv), F32), pltpu.VMEM((seq_len, wv), F32)]
    if emit_state:
        out_specs += [state_spec, state_spec]
        out_shape += [jax.ShapeDtypeStruct((batch, GLA_HEADS, GLA_HEAD_K, GLA_HEAD_V), F32)] * 2
    else:
        scratch_shapes = [pltpu.VMEM(state_shape, F32), pltpu.VMEM(state_shape, F32)] + scratch_shapes
    state_bytes = _nbytes(state_shape, F32)
    blocks = (2 * _nbytes((seq_len, wk), BF16) + 3 * _nbytes((seq_len, wv), BF16)
              + _nbytes((seq_len, LANES), F32) + 2 * state_bytes)
    scratch = 2 * _nbytes((seq_len, wv), F32) + (0 if emit_state else 2 * state_bytes)
    outs = pl.pallas_call(
        functools.partial(_gla_kernel, seq_len=seq_len, heads=heads, has_init=s0 is not None,
                          emit_state=emit_state),
        grid=(batch, GLA_HEADS // heads),
        in_specs=in_specs,
        out_specs=out_specs,
        out_shape=out_shape,
        scratch_shapes=scratch_shapes,
        compiler_params=pltpu.CompilerParams(
            dimension_semantics=("parallel", "arbitrary"),
            vmem_limit_bytes=_vmem_limit(blocks, scratch_bytes=scratch, temp_bytes=8 << 20)),
        name="gla",
    )(*args)
    og = outs[0].reshape(batch * seq_len, GLA_DV)
    return (og, outs[1], outs[2]) if emit_state else (og, None, None)


def _conv_kernel(ua_ref, ug_ref, cw_ref, cb_ref, lng_ref, lnb_ref, o_ref, pad_ref, shift_ref, conv_ref, *, seg_len):
    n_seg = CONV_TILE // seg_len
    stride = seg_len + 2 * CONV_HALO
    pad_rows = n_seg * stride
    rows = min(seg_len, 64)
    glu = ua_ref[...].astype(F32) * _sigmoid(ug_ref[...].astype(F32))
    halo = jnp.zeros((CONV_HALO, CONV_W), F32)
    for s in range(n_seg):
        base = s * stride
        pad_ref[base:base + CONV_HALO, :] = halo
        pad_ref[base + CONV_HALO:base + CONV_HALO + seg_len, :] = glu[s * seg_len:(s + 1) * seg_len, :]
        pad_ref[base + CONV_HALO + seg_len:base + stride, :] = halo
    pad_ref[pad_rows:pad_rows + SUBLANES, :] = jnp.zeros((SUBLANES, CONV_W), F32)
    for r in range(1, SUBLANES):
        shift_ref[r - 1, :, :] = pad_ref[r:pad_rows + r, :]

    def lane_block(cb, carry):
        lanes = pl.ds(pl.multiple_of(cb * LANES, LANES), LANES)
        bias = cb_ref[:, lanes]
        for s in range(n_seg):
            for rb in range(seg_len // rows):
                first = s * stride + CONV_HALO - CONV_K // 2 + rb * rows
                acc = jnp.zeros((rows, LANES), F32)
                for tap in range(CONV_K):
                    aligned, r = divmod(first + tap, SUBLANES)
                    at = pl.ds(aligned * SUBLANES, rows)
                    window = pad_ref[at, lanes] if r == 0 else shift_ref[r - 1, at, lanes]
                    acc = acc + window * cw_ref[tap:tap + 1, lanes]
                conv_ref[pl.ds(s * seg_len + rb * rows, rows), lanes] = acc + bias
        return carry

    lax.fori_loop(0, CONV_W // LANES, lane_block, 0)

    u = conv_ref[...]
    mu = jnp.mean(u, axis=-1, keepdims=True)
    uc = u - mu
    var = jnp.mean(uc * uc, axis=-1, keepdims=True)
    y = uc * lax.rsqrt(var + EPS) * lng_ref[...] + lnb_ref[...]
    o_ref[...] = _silu(y).astype(o_ref.dtype)


def _conv_module(proj, conv_w, conv_b, ln_g, ln_b, seg_len):
    m = proj.shape[0]
    n_seg = CONV_TILE // seg_len
    pad_rows = n_seg * (seg_len + 2 * CONV_HALO)
    row = lambda v: v.reshape(1, CONV_W)
    blocks = 3 * _nbytes((CONV_TILE, CONV_W), BF16)
    scratch = SUBLANES * _nbytes((pad_rows + SUBLANES, CONV_W), F32) + _nbytes((CONV_TILE, CONV_W), F32)
    return pl.pallas_call(
        functools.partial(_conv_kernel, seg_len=seg_len),
        grid=(m // CONV_TILE,),
        in_specs=[
            pl.BlockSpec((CONV_TILE, CONV_W), lambda i: (i, COL_UA // CONV_W)),
            pl.BlockSpec((CONV_TILE, CONV_W), lambda i: (i, COL_UG // CONV_W)),
            pl.BlockSpec((CONV_K, CONV_W), lambda i: (0, 0)),
            pl.BlockSpec((1, CONV_W), lambda i: (0, 0)),
            pl.BlockSpec((1, CONV_W), lambda i: (0, 0)),
            pl.BlockSpec((1, CONV_W), lambda i: (0, 0)),
        ],
        out_specs=pl.BlockSpec((CONV_TILE, CONV_W), lambda i: (i, 0)),
        out_shape=jax.ShapeDtypeStruct((m, CONV_W), BF16),
        scratch_shapes=[pltpu.VMEM((pad_rows + SUBLANES, CONV_W), F32),
                        pltpu.VMEM((SUBLANES - 1, pad_rows, CONV_W), F32),
                        pltpu.VMEM((CONV_TILE, CONV_W), F32)],
        compiler_params=pltpu.CompilerParams(
            dimension_semantics=("parallel",),
            vmem_limit_bytes=_vmem_limit(blocks, scratch_bytes=scratch,
                                         temp_bytes=4 * _nbytes((CONV_TILE, CONV_W), F32))),
        name="conv_module",
    )(proj, proj, conv_w, row(conv_b), row(ln_g), row(ln_b))


def _trunk(x, mods, row_of_tile, batch, seq_len, conv_seg, wts, s0, emit_state):
    h = _norm_mod(x, wts["norm_mix_g"], mods, 0, 1, row_of_tile)
    proj_att = _mm(h, wts["w_att"], BF16, 1024, 1024, "in_proj_att")
    proj_mix = _mm(h, wts["w_mix"], BF16, 1024, 1024, "in_proj_mix")
    a_lr = _mm(h, wts["w_decay"], F32, 1024, LANES, "decay_proj")
    og, s_f, s_b = _gla(proj_att, a_lr, wts["w_a2f"], wts["b_af"], wts["w_a2b"], wts["b_ab"], wts["gla_norm_g"],
                        batch, seq_len, s0=s0, emit_state=emit_state)
    uc = _conv_module(proj_mix, wts["conv_w"], wts["conv_b"], wts["conv_ln_g"], wts["conv_ln_b"], conv_seg)
    mix = _merge(og, uc, proj_mix, wts["w_gla_o"], wts["w_conv_o"])
    x1 = _mm_resid(mix, wts["w_out"], x, mods, 2, row_of_tile, 1024, 1024, D_MODEL, "out_proj")
    h2 = _norm_mod(x1, wts["norm_ffn_g"], mods, 3, 4, row_of_tile)
    hid = _ffn_glu(h2, wts["w_ffn1"], wts["w_ffn3"])
    x2 = _mm_resid(hid, wts["w_ffn2"], x1, mods, 5, row_of_tile, 512, 512, FFN_PAD, "ffn_down")
    return _final_norm(x2, wts["final_norm_g"]), s_f, s_b


def kernel(x_prompt, x_sample, state_gla_fwd, state_gla_bwd, c, c_ctx, norm_mix_g, w_ada, b_ada, w_in, w_a2_fwd, b_a_fwd, w_a2_bwd, b_a_bwd, gla_norm_g, w_gla_o, conv_w, conv_b, conv_ln_g, conv_ln_b, w_conv_o, w_out, norm_ffn_g, w_ffn1, w_ffn3, w_ffn2, final_norm_g):
    n_req, seq, _ = x_prompt.shape
    dec_batch, dec_seq, _ = x_sample.shape
    depth = w_in.shape[0]
    assert depth == 1 and dec_batch + 1 <= MOD_ROWS

    w_in_t = jnp.transpose(w_in[0])
    w_att = _cast_transposed(w_in_t, 0, HALF_WIDTH)
    w_mix = _cast_transposed(w_in_t, DECAY_COL0 + 2 * GATE_RANK, HALF_WIDTH)
    w_decay = _cast_transposed(w_in_t, DECAY_COL0, LANES, tr=LANES, keep=2 * GATE_RANK)
    pad_ffn = FFN_PAD - FFN_HIDDEN
    decay_rows = lambda w, first: jnp.pad(w.astype(BF16), ((first, LANES - GATE_RANK - first), (0, 0)))
    wts = {
        "w_att": w_att, "w_mix": w_mix, "w_decay": w_decay,
        "w_a2f": decay_rows(w_a2_fwd[0], 0),
        "w_a2b": decay_rows(w_a2_bwd[0], GATE_RANK),
        "b_af": b_a_fwd[0].reshape(1, GLA_DK),
        "b_ab": b_a_bwd[0].reshape(1, GLA_DK),
        "gla_norm_g": gla_norm_g[0].reshape(1, GLA_HEAD_V),
        "w_gla_o": w_gla_o[0].astype(BF16),
        "conv_w": conv_w[0], "conv_b": conv_b[0], "conv_ln_g": conv_ln_g[0], "conv_ln_b": conv_ln_b[0],
        "w_conv_o": w_conv_o[0].astype(BF16),
        "w_out": w_out[0].astype(BF16),
        "w_ffn1": _cast_pad(w_ffn1[0], (D_MODEL, FFN_PAD), (D_MODEL, pad_ffn)),
        "w_ffn3": _cast_pad(w_ffn3[0], (D_MODEL, FFN_PAD), (D_MODEL, pad_ffn)),
        "w_ffn2": _cast_pad(w_ffn2[0], (FFN_PAD, D_MODEL), (pad_ffn, D_MODEL)),
        "norm_mix_g": norm_mix_g[0], "norm_ffn_g": norm_ffn_g[0], "final_norm_g": final_norm_g,
    }

    cond = jnp.concatenate([c_ctx[None, :], c, jnp.zeros((MOD_ROWS - 1 - dec_batch, D_MODEL), F32)], axis=0)
    mods = _ada(cond, w_ada[0], b_ada[0]).reshape(MOD_ROWS, 1, N_MOD * D_MODEL)

    ctx_row = lambda i, tm: 0
    lat_row = lambda i, tm: 1 + (i * tm) // dec_seq

    y_p, s_f, s_b = _trunk(x_prompt.reshape(n_req * seq, D_MODEL), mods, ctx_row, n_req, seq, seq, wts,
                           None, True)
    y_s, _, _ = _trunk(x_sample.reshape(dec_batch * dec_seq, D_MODEL), mods, lat_row, dec_batch, dec_seq, GRID_W,
                       wts, (state_gla_fwd[:, 0], state_gla_bwd[:, 0]), False)
    state_shape = (n_req, depth, GLA_HEADS, GLA_HEAD_K, GLA_HEAD_V)
    return (y_p.reshape(n_req, seq, D_MODEL), y_s.reshape(dec_batch, dec_seq, D_MODEL),
            s_f.reshape(state_shape), s_b.reshape(state_shape))
```
